```python
import math
import jax, jax.numpy as jnp
from jax import lax
import numpy as np

D_MODEL = 1024
BATCH = 2
SEQ = 8192
DEPTH = 2
DEC_BATCH = 8
DEC_SEQ = 16
PAST_LEN = 4096

CHUNK = 64
HEAD_DIM = 64
A_WIDTH = 3 * D_MODEL // 8
B_WIDTH = 3 * D_MODEL // 8
C_WIDTH = D_MODEL - A_WIDTH - B_WIDTH
A_HEADS = A_WIDTH // HEAD_DIM
B_HEADS = B_WIDTH // HEAD_DIM
C_GROUPS = C_WIDTH // HEAD_DIM
MIX_WIDTH = A_WIDTH + B_WIDTH + C_WIDTH
PROJ_SIZES = (A_WIDTH, A_WIDTH, A_WIDTH, B_WIDTH, B_WIDTH, B_WIDTH, B_WIDTH, C_WIDTH, C_WIDTH)
IN_WIDTH = sum(PROJ_SIZES)
Q_BLOCK = 128
C_CHUNK = 128
FF_DENSE = 2816
N_EXPERTS = 8
TOP_K = 2
FF_EXPERT = 1024
N_DENSE = (DEPTH + 1) // 2
N_MOE = DEPTH // 2
EPS = 1e-6

kernel_name = "hybrid_stickbreak_hgrn2_sgu_stream_step"


def rmsnorm(x, g):
    xf = x.astype(jnp.float32)
    y = xf * lax.rsqrt(jnp.mean(xf * xf, axis=-1, keepdims=True) + EPS)
    return (y * g.astype(jnp.float32)).astype(x.dtype)


def sb_block(q, k, v, q_pos, k_pos):
    z = jnp.einsum('bqhd,bkhd->bhqk', q, k).astype(jnp.float32) * (HEAD_DIM ** -0.5)
    mask = k_pos[None, :] < q_pos[:, None]
    log_keep = jnp.where(mask, jax.nn.log_sigmoid(-z), 0.0)
    after = lax.cumsum(log_keep, axis=3, reverse=True) - log_keep
    w = jnp.where(mask, jnp.exp(jax.nn.log_sigmoid(z) + after), 0.0)
    return jnp.einsum('bhqk,bkhd->bqhd', w.astype(v.dtype), v)


def sb_prompt(q, k, v):
    bsz, s, h, dh = q.shape
    nb = s // Q_BLOCK
    pos = jnp.arange(s, dtype=jnp.int32)
    qb = q.reshape(bsz, nb, Q_BLOCK, h, dh).swapaxes(0, 1)
    qpos = pos.reshape(nb, Q_BLOCK)
    out = lax.map(lambda a: sb_block(a[0], k, v, a[1], pos), (qb, qpos))
    return out.swapaxes(0, 1).reshape(bsz, s, h, dh)


def hgrn_lower_bounds(lb_param):
    cs = jnp.cumsum(jax.nn.softmax(lb_param.astype(jnp.float32), axis=0), axis=0)
    return cs - cs[0:1]


def hgrn2(qr, fr, ir, gr, lb, o_gain, s0, blk):
    bsz, t, _ = qr.shape
    z = fr.astype(jnp.float32)
    log_f = jnp.logaddexp(jnp.log(lb), jnp.log1p(-lb) + jax.nn.log_sigmoid(z))
    k = (1.0 - lb) * jax.nn.sigmoid(-z)
    q = jax.nn.silu(qr.astype(jnp.float32))
    v = ir.astype(jnp.float32)
    nb = t // blk
    to_blocks = lambda a: a.reshape(bsz, nb, blk, B_HEADS, HEAD_DIM).swapaxes(0, 1)
    tril = jnp.tril(jnp.ones((blk, blk), dtype=bool))

    def step(S, inp):
        qb, kb, vb, gb = inp
        b = jnp.cumsum(gb, axis=1)
        o_inter = jnp.einsum('blhk,bhkv->blhv', qb * jnp.exp(b), S)
        diff = b[:, :, None] - b[:, None]
        dec = jnp.exp(jnp.where(tril[None, :, :, None, None], diff, -jnp.inf))
        att = jnp.einsum('btshk,bthk,bshk->bhts', dec, qb, kb)
        o_intra = jnp.einsum('bhts,bshv->bthv', att, vb)
        bl = b[:, -1]
        S = jnp.exp(bl)[..., None] * S + jnp.einsum('bshk,bshv->bhkv', kb * jnp.exp(bl[:, None] - b), vb)
        return S, o_inter + o_intra

    S, o = lax.scan(step, s0.astype(jnp.float32), (to_blocks(q), to_blocks(k), to_blocks(v), to_blocks(log_f)))
    o = o.swapaxes(0, 1).reshape(bsz, t, B_HEADS, HEAD_DIM)
    o = rmsnorm(o, o_gain).reshape(bsz, t, B_WIDTH) * jax.nn.silu(gr.astype(jnp.float32))
    return o.astype(qr.dtype), S.astype(qr.dtype)


def sgu(ur, vr, v_gain, w_s, b_s, lc):
    bsz, t, _ = ur.shape
    nc = t // lc
    u = jax.nn.gelu(ur).reshape(bsz, nc, lc, C_GROUPS, HEAD_DIM)
    vn = rmsnorm(jax.nn.gelu(vr).reshape(bsz, nc, lc, C_GROUPS, HEAD_DIM), v_gain)
    w = w_s[:, :lc, :lc] * jnp.tril(jnp.ones((lc, lc), dtype=w_s.dtype))
    mixed = jnp.einsum('gts,bnsgc->bntgc', w, vn) + b_s[:, :lc].T[None, None, :, :, None]
    y = u * mixed
    return y.reshape(bsz, t, C_WIDTH), vn.reshape(bsz, t, C_WIDTH)


def swiglu(h, wg, wu, wd):
    return (jax.nn.silu(h @ wg) * (h @ wu)) @ wd


def moe_swiglu(h, wr, wg, wu, wd):
    logits = (h @ wr).astype(jnp.float32)
    top_val, top_idx = lax.top_k(logits, TOP_K)
    gates = jax.nn.softmax(top_val, axis=-1)
    combine = jnp.sum(jax.nn.one_hot(top_idx, N_EXPERTS, dtype=jnp.float32) * gates[..., None], axis=-2)
    out = jnp.zeros_like(h)
    for e in range(N_EXPERTS):
        out = out + combine[..., e:e + 1].astype(h.dtype) * swiglu(h, wg[e], wu[e], wd[e])
    return out


def setup_inputs(seed: int = 0) -> dict:
    key = jax.random.key(seed)
    ks = jax.random.split(key, 24)
    n = lambda i, shape, s=1.0: jax.random.normal(ks[i], shape, jnp.float32) * s
    return {
        "x_prompt": n(0, (BATCH, SEQ, D_MODEL)),
        "x_sample": n(1, (DEC_BATCH, DEC_SEQ, D_MODEL)),
        "cache_k": n(2, (DEPTH, DEC_BATCH, PAST_LEN, A_HEADS, HEAD_DIM)),
        "cache_v": n(3, (DEPTH, DEC_BATCH, PAST_LEN, A_HEADS, HEAD_DIM)),
        "state_b": n(4, (DEPTH, DEC_BATCH, B_HEADS, HEAD_DIM, HEAD_DIM), 0.5),
        "w_in": n(5, (DEPTH, D_MODEL, IN_WIDTH), D_MODEL ** -0.5),
        "w_out": n(6, (DEPTH, MIX_WIDTH, D_MODEL), MIX_WIDTH ** -0.5),
        "norm_mix": 1.0 + n(7, (DEPTH, D_MODEL), 0.01),
        "norm_ffn": 1.0 + n(8, (DEPTH, D_MODEL), 0.01),
        "q_norm": 1.0 + n(9, (DEPTH, HEAD_DIM), 0.01),
        "k_norm": 1.0 + n(10, (DEPTH, HEAD_DIM), 0.01),
        "hgrn_lb": n(11, (DEPTH, B_WIDTH)),
        "hgrn_out_norm": 1.0 + n(12, (DEPTH, HEAD_DIM), 0.01),
        "sgu_norm": 1.0 + n(13, (DEPTH, HEAD_DIM), 0.01),
        "sgu_w": n(14, (DEPTH, C_GROUPS, C_CHUNK, C_CHUNK), 0.5 * C_CHUNK ** -0.5),
        "sgu_b": 1.0 + n(15, (DEPTH, C_GROUPS, C_CHUNK), 0.01),
        "ffn_w_gate": n(16, (N_DENSE, D_MODEL, FF_DENSE), D_MODEL ** -0.5),
        "ffn_w_up": n(17, (N_DENSE, D_MODEL, FF_DENSE), D_MODEL ** -0.5),
        "ffn_w_down": n(18, (N_DENSE, FF_DENSE, D_MODEL), FF_DENSE ** -0.5),
        "moe_router": n(19, (N_MOE, D_MODEL, N_EXPERTS), D_MODEL ** -0.5),
        "moe_w_gate": n(20, (N_MOE, N_EXPERTS, D_MODEL, FF_EXPERT), D_MODEL ** -0.5),
        "moe_w_up": n(21, (N_MOE, N_EXPERTS, D_MODEL, FF_EXPERT), D_MODEL ** -0.5),
        "moe_w_down": n(22, (N_MOE, N_EXPERTS, FF_EXPERT, D_MODEL), FF_EXPERT ** -0.5),
    }


def reference(x_prompt, x_sample, cache_k, cache_v, state_b, w_in, w_out, norm_mix, norm_ffn,
              q_norm, k_norm, hgrn_lb, hgrn_out_norm, sgu_norm, sgu_w, sgu_b,
              ffn_w_gate, ffn_w_up, ffn_w_down, moe_router, moe_w_gate, moe_w_up, moe_w_down):
    split_idx = []
    run = 0
    for sz in PROJ_SIZES[:-1]:
        run += sz
        split_idx.append(run)
    lb_all = hgrn_lower_bounds(hgrn_lb)

    def mixer(xn, l, hist):
        bsz, t, _ = xn.shape
        aq, ak, av, bq, bf, bi, bg, cu, cvr = jnp.split(xn @ w_in[l], split_idx, axis=-1)
        q = rmsnorm(aq.reshape(bsz, t, A_HEADS, HEAD_DIM), q_norm[l])
        k = rmsnorm(ak.reshape(bsz, t, A_HEADS, HEAD_DIM), k_norm[l])
        v = av.reshape(bsz, t, A_HEADS, HEAD_DIM)
        if hist is None:
            oa = sb_prompt(q, k, v)
            s0 = jnp.zeros((bsz, B_HEADS, HEAD_DIM, HEAD_DIM), jnp.float32)
            blk, lc = CHUNK, C_CHUNK
        else:
            ck, cvv, s0 = hist
            past = ck.shape[1]
            k_all = jnp.concatenate([ck, k], axis=1)
            v_all = jnp.concatenate([cvv, v], axis=1)
            q_pos = past + jnp.arange(t, dtype=jnp.int32)
            k_pos = jnp.arange(past + t, dtype=jnp.int32)
            oa = sb_block(q, k_all, v_all, q_pos, k_pos)
            blk, lc = t, t
        ob, s_new = hgrn2(bq, bf, bi, bg, lb_all[l], hgrn_out_norm[l], s0, blk)
        oc, vn = sgu(cu, cvr, sgu_norm[l], sgu_w[l], sgu_b[l], lc)
        o = jnp.concatenate([oa.reshape(bsz, t, A_WIDTH), ob, oc], axis=-1)
        return o @ w_out[l], k, v, s_new, vn

    def ffn(hn, l):
        i = l // 2
        if l % 2 == 0:
            return swiglu(hn, ffn_w_gate[i], ffn_w_up[i], ffn_w_down[i])
        return moe_swiglu(hn, moe_router[i], moe_w_gate[i], moe_w_up[i], moe_w_down[i])

    yp, ys = x_prompt, x_sample
    kp, vp, sp, ks_, vs_, ss_, cs_ = [], [], [], [], [], [], []
    for l in range(DEPTH):
        m, k_, v_, s_, _ = mixer(rmsnorm(yp, norm_mix[l]), l, None)
        yp = yp + m
        yp = yp + ffn(rmsnorm(yp, norm_ffn[l]), l)
        kp.append(k_); vp.append(v_); sp.append(s_)
        m, k_, v_, s_, vn = mixer(rmsnorm(ys, norm_mix[l]), l, (cache_k[l], cache_v[l], state_b[l]))
        ys = ys + m
        ys = ys + ffn(rmsnorm(ys, norm_ffn[l]), l)
        ks_.append(k_); vs_.append(v_); ss_.append(s_); cs_.append(vn)
    return (yp, ys, jnp.stack(kp), jnp.stack(vp), jnp.stack(sp), jnp.stack(ks_), jnp.stack(vs_), jnp.stack(ss_), jnp.stack(cs_))
```

```python
import functools

import jax
import jax.numpy as jnp
from jax import lax
from jax.experimental import pallas as pl
from jax.experimental.pallas import tpu as pltpu

F32 = jnp.float32
BF16 = jnp.bfloat16

HEAD_DIM = 64
LANES = 128
EPS = 1e-6
N_EXPERTS = 8
TOP_K = 2
VMEM_LIMIT = 56 * 1024 * 1024

ATT_BLOCK = 128
ATT_DEAD = -120.0
HG_STEP = 16
NEG_BIG = -1e30


def _dot(a, b):
    return jnp.dot(a, b, preferred_element_type=F32)


def _dot_nt(a, b):
    return lax.dot_general(a, b, (((1,), (1,)), ((), ())), preferred_element_type=F32)


def _split2(x):
    hi = x.astype(BF16)
    lo = (x - hi.astype(F32)).astype(BF16)
    return hi, lo


def _split3(x):
    hi = x.astype(BF16)
    r = x - hi.astype(F32)
    mid = r.astype(BF16)
    lo = (r - mid.astype(F32)).astype(BF16)
    return hi, mid, lo


def _dot_x2(x, m):
    hi, lo = _split2(x)
    return _dot(hi, m) + _dot(lo, m)


def _ldot_x3(m, x):
    hi, mid, lo = _split3(x)
    return _dot(m, hi) + _dot(m, mid) + _dot(m, lo)


def _log1p_exp_neg_abs(z):
    return jnp.log1p(jnp.exp(-jnp.abs(z)))


def _group_rmsnorm(a, ind, gain_row):
    ms = _dot_x2(a * a, ind) * (1.0 / HEAD_DIM)
    return a * lax.rsqrt(ms + EPS) * gain_row


def _group_indicator(width):
    g = jnp.arange(width, dtype=jnp.int32) // HEAD_DIM
    return (g[:, None] == g[None, :]).astype(BF16)


def _const_spec(shape):
    nd = len(shape)
    return pl.BlockSpec(shape, lambda *_: (0,) * nd)


def _params(*sem):
    return pltpu.CompilerParams(dimension_semantics=sem, vmem_limit_bytes=VMEM_LIMIT)


def _inproj_kernel(x_ref, g_ref, w_ref, qg_ref, kg_ref, ind_ref,
                   q_ref, k_ref, v_ref, kb_ref, vb_ref, b_ref, c_ref, *, aw, bw):
    x = x_ref[...]
    ms = jnp.mean(x * x, axis=-1, keepdims=True)
    xn = (x * lax.rsqrt(ms + EPS) * g_ref[...]).astype(BF16)
    ind = ind_ref[...]
    aq = _dot(xn, w_ref[:, 0:aw])
    q = _group_rmsnorm(aq, ind, qg_ref[...])
    q_ref[...] = (q * (HEAD_DIM ** -0.5)).astype(BF16)
    ak = _dot(xn, w_ref[:, aw:2 * aw])
    k = _group_rmsnorm(ak, ind, kg_ref[...])
    k_ref[...] = k
    kb_ref[...] = k.astype(BF16)
    v = _dot(xn, w_ref[:, 2 * aw:3 * aw])
    v_ref[...] = v
    vb_ref[...] = v.astype(BF16)
    b_ref[...] = _dot(xn, w_ref[:, 3 * aw:3 * aw + 4 * bw])
    c_ref[...] = _dot(xn, w_ref[:, 3 * aw + 4 * bw:])


def _inproj(x, g, w, qg, kg, *, aw, bw, cw, tm):
    m, d = x.shape
    tm = min(tm, m)
    nw = w.shape[1]
    row = lambda width: pl.BlockSpec((tm, width), lambda i: (i, 0))
    return pl.pallas_call(
        functools.partial(_inproj_kernel, aw=aw, bw=bw),
        grid=(m // tm,),
        in_specs=[row(d), _const_spec((1, d)), _const_spec((d, nw)),
                  _const_spec((1, aw)), _const_spec((1, aw)), _const_spec((aw, aw))],
        out_specs=[row(aw), row(aw), row(aw), row(aw), row(aw), row(4 * bw), row(2 * cw)],
        out_shape=[jax.ShapeDtypeStruct((m, aw), BF16),
                   jax.ShapeDtypeStruct((m, aw), F32),
                   jax.ShapeDtypeStruct((m, aw), F32),
                   jax.ShapeDtypeStruct((m, aw), BF16),
                   jax.ShapeDtypeStruct((m, aw), BF16),
                   jax.ShapeDtypeStruct((m, 4 * bw), F32),
                   jax.ShapeDtypeStruct((m, 2 * cw), F32)],
        compiler_params=_params("parallel"),
        name="inproj",
    )(x, g.reshape(1, d), w, qg, kg, _group_indicator(aw))


def _head_masks(rows):
    lane = lax.broadcasted_iota(jnp.int32, (rows, LANES), 1)
    return lane < HEAD_DIM


def _sb_block(qh, kblk, vblk, carry_ref, acc_ref, cum_mat, ones_mat, low, mask):
    tk = kblk.shape[0]
    for h, q in enumerate(qh):
        p = h // 2
        lanes = slice(p * LANES, (p + 1) * LANES)
        z = _dot_nt(q, kblk[:, lanes])
        sp = _log1p_exp_neg_abs(z)
        log_beta = jnp.minimum(z, 0.0) - sp
        log_keep = jnp.minimum(-z, 0.0) - sp
        if mask is not None:
            log_keep = jnp.where(mask, log_keep, 0.0)
        hi, lo = _split2(log_keep)
        after = _dot(hi, cum_mat) + _dot(lo, cum_mat)
        total = _dot(hi, ones_mat) + _dot(lo, ones_mat)
        carry = carry_ref[h]
        if tk == LANES:
            after = after + carry
        else:
            after = after + carry[:, :tk]
        w = jnp.exp(log_beta + after)
        if mask is not None:
            w = jnp.where(mask, w, 0.0)
        pv = _dot(w.astype(BF16), vblk[:, lanes])
        keep = low if h % 2 == 0 else jnp.logical_not(low)
        acc_ref[:, lanes] += jnp.where(keep, pv, 0.0)
        carry_ref[h] = carry + total


def _split_heads(q, low):
    qh = []
    for p in range(q.shape[1] // LANES):
        qp = q[:, p * LANES:(p + 1) * LANES]
        qh.append(jnp.where(low, qp, jnp.zeros_like(qp)))
        qh.append(jnp.where(low, jnp.zeros_like(qp), qp))
    return qh


def _carry_max(carry_ref):
    m = carry_ref[0]
    for h in range(1, carry_ref.shape[0]):
        m = jnp.maximum(m, carry_ref[h])
    return jnp.max(m)


def _attn_prompt_kernel(q_ref, k_ref, v_ref, cum_ref, ones_ref, o_ref, carry_ref, acc_ref, *, blk):
    i = pl.program_id(1)
    low = _head_masks(blk)
    qh = _split_heads(q_ref[...], low)
    carry_ref[...] = jnp.zeros_like(carry_ref)
    acc_ref[...] = jnp.zeros_like(acc_ref)
    cum = cum_ref[...]
    ones = ones_ref[...]
    r = lax.broadcasted_iota(jnp.int32, (blk, blk), 0)
    c = lax.broadcasted_iota(jnp.int32, (blk, blk), 1)
    start = pl.multiple_of(i * blk, blk)
    _sb_block(qh, k_ref[pl.ds(start, blk), :], v_ref[pl.ds(start, blk), :],
              carry_ref, acc_ref, cum, ones, low, c < r)

    def cond(state):
        kb, live = state
        return jnp.logical_and(kb >= 0, live > ATT_DEAD)

    def body(state):
        kb, _ = state
        s = pl.multiple_of(kb * blk, blk)
        _sb_block(qh, k_ref[pl.ds(s, blk), :], v_ref[pl.ds(s, blk), :],
                  carry_ref, acc_ref, cum, ones, low, None)
        return kb - 1, _carry_max(carry_ref)

    lax.while_loop(cond, body, (i - 1, _carry_max(carry_ref)))
    o_ref[...] = acc_ref[...].astype(o_ref.dtype)


def _cum_matrix(n):
    j = jnp.arange(n, dtype=jnp.int32)
    return (j[:, None] > j[None, :]).astype(BF16)


def _attn_prompt(q, kb, vb, *, bsz, seq):
    m, aw = q.shape
    blk = ATT_BLOCK
    nq = seq // blk
    nh = aw // HEAD_DIM
    return pl.pallas_call(
        functools.partial(_attn_prompt_kernel, blk=blk),
        grid=(bsz, nq),
        in_specs=[pl.BlockSpec((blk, aw), lambda b, i: (b * nq + i, 0)),
                  pl.BlockSpec((seq, aw), lambda b, i: (b, 0)),
                  pl.BlockSpec((seq, aw), lambda b, i: (b, 0)),
                  _const_spec((blk, blk)), _const_spec((blk, LANES))],
        out_specs=pl.BlockSpec((blk, aw), lambda b, i: (b * nq + i, 0)),
        out_shape=jax.ShapeDtypeStruct((m, aw), BF16),
        scratch_shapes=[pltpu.VMEM((nh, blk, LANES), F32), pltpu.VMEM((blk, aw), F32)],
        compiler_params=_params("parallel", "arbitrary"),
        name="attn_prompt",
    )(q, kb, vb, _cum_matrix(blk), jnp.ones((blk, LANES), BF16))


def _attn_sample_kernel(q_ref, kn_ref, vn_ref, ck_ref, cv_ref, cum_new_ref, ones_new_ref,
                        cum_ref, ones_ref, o_ref, carry_ref, acc_ref, *, t, blk, past):
    low = _head_masks(t)
    qh = _split_heads(q_ref[...], low)
    carry_ref[...] = jnp.zeros_like(carry_ref)
    acc_ref[...] = jnp.zeros_like(acc_ref)
    r = lax.broadcasted_iota(jnp.int32, (t, t), 0)
    c = lax.broadcasted_iota(jnp.int32, (t, t), 1)
    _sb_block(qh, kn_ref[...], vn_ref[...], carry_ref, acc_ref,
              cum_new_ref[...], ones_new_ref[...], low, c < r)
    cum = cum_ref[...]
    ones = ones_ref[...]

    def cond(state):
        kb, live = state
        return jnp.logical_and(kb >= 0, live > ATT_DEAD)

    def body(state):
        kb, _ = state
        s = pl.multiple_of(kb * blk, blk)
        _sb_block(qh, ck_ref[pl.ds(s, blk), :].astype(BF16), cv_ref[pl.ds(s, blk), :].astype(BF16),
                  carry_ref, acc_ref, cum, ones, low, None)
        return kb - 1, _carry_max(carry_ref)

    lax.while_loop(cond, body, (past // blk - 1, _carry_max(carry_ref)))
    o_ref[...] = acc_ref[...].astype(o_ref.dtype)


def _attn_sample(q, kb, vb, ck, cv, *, bsz, t, past):
    m, aw = q.shape
    blk = ATT_BLOCK
    nh = aw // HEAD_DIM
    new = pl.BlockSpec((t, aw), lambda b: (b, 0))
    old = pl.BlockSpec((past, aw), lambda b: (b, 0))
    return pl.pallas_call(
        functools.partial(_attn_sample_kernel, t=t, blk=blk, past=past),
        grid=(bsz,),
        in_specs=[new, new, new, old, old,
                  _const_spec((t, t)), _const_spec((t, LANES)),
                  _const_spec((blk, blk)), _const_spec((blk, LANES))],
        out_specs=new,
        out_shape=jax.ShapeDtypeStruct((m, aw), BF16),
        scratch_shapes=[pltpu.VMEM((nh, t, LANES), F32), pltpu.VMEM((t, aw), F32)],
        compiler_params=_params("parallel"),
        name="attn_sample",
    )(q, kb, vb, ck, cv, _cum_matrix(t), jnp.ones((t, LANES), BF16),
      _cum_matrix(blk), jnp.ones((blk, LANES), BF16))


def _hgrn_step(blk, lbp, gain, tril, ind, bd_mask, st_ref):
    bw = gain.shape[1]
    n = blk.shape[0]
    qr = blk[:, 0:bw]
    z = blk[:, bw:2 * bw]
    v = blk[:, 2 * bw:3 * bw]
    gr = blk[:, 3 * bw:4 * bw]
    log_lb, log1m_lb, one_m_lb = lbp[0:1], lbp[1:2], lbp[2:3]

    sp = _log1p_exp_neg_abs(z)
    cc = log1m_lb + (jnp.minimum(z, 0.0) - sp)
    log_f = jnp.maximum(log_lb, cc) + jnp.log1p(jnp.exp(-jnp.abs(log_lb - cc)))
    k = one_m_lb * jnp.exp(jnp.minimum(-z, 0.0) - sp)
    q = qr * jax.nn.sigmoid(qr)

    b = _ldot_x3(tril, log_f)
    b_last = b[n - 1:n]
    qb = (q * jnp.exp(b)).astype(BF16)
    kd = k * jnp.exp(b_last - b)
    decay = jnp.exp(b_last)

    row = lax.broadcasted_iota(jnp.int32, (n, bw), 0)
    parts = []
    for s in range(n):
        e = jnp.exp(jnp.where(row >= s, b - b[s:s + 1], NEG_BIG))
        parts.append(((q * e) * k[s:s + 1]).astype(BF16))
    att = _dot(jnp.concatenate(parts, axis=0), ind)
    o = jnp.zeros((n, bw), F32)
    for s in range(n):
        o = o + att[s * n:(s + 1) * n] * v[s:s + 1]

    outs = []
    for p in range(bw // LANES):
        lanes = slice(p * LANES, (p + 1) * LANES)
        st = st_ref[p]
        outs.append(_dot_nt(qb[:, lanes], st.astype(BF16)))
        upd = lax.dot_general(v[:, lanes].astype(BF16), kd[:, lanes].astype(BF16),
                              (((0,), (0,)), ((), ())), preferred_element_type=F32)
        st_ref[p] = st * decay[:, lanes] + jnp.where(bd_mask, upd, 0.0)
    o = o + jnp.concatenate(outs, axis=1)

    ms = _dot_x2(o * o, ind) * (1.0 / HEAD_DIM)
    o = o * lax.rsqrt(ms + EPS) * gain
    return o * (gr * jax.nn.sigmoid(gr))


def _hgrn_kernel(b_ref, lbp_ref, gain_ref, tril_ref, ind_ref, st0_ref, o_ref, st_out_ref, st_ref,
                 *, steps, unroll):
    i = pl.program_id(1)

    @pl.when(i == 0)
    def _():
        st_ref[...] = st0_ref[...]

    lbp = lbp_ref[...]
    gain = gain_ref[...]
    tril = tril_ref[...]
    ind = ind_ref[...]
    r = lax.broadcasted_iota(jnp.int32, (LANES, LANES), 0) // HEAD_DIM
    c = lax.broadcasted_iota(jnp.int32, (LANES, LANES), 1) // HEAD_DIM
    bd_mask = r == c

    def body(j, _):
        for u in range(unroll):
            rows = pl.ds(pl.multiple_of((j * unroll + u) * HG_STEP, HG_STEP), HG_STEP)
            o = _hgrn_step(b_ref[rows, :], lbp, gain, tril, ind, bd_mask, st_ref)
            o_ref[rows, :] = o.astype(o_ref.dtype)
        return 0

    lax.fori_loop(0, steps // unroll, body, 0)

    @pl.when(i == pl.num_programs(1) - 1)
    def _():
        st_out_ref[...] = st_ref[...]


def _hgrn(bproj, lbp, gain, st0, *, bsz, t, tc):
    m, w4 = bproj.shape
    bw = w4 // 4
    tc = min(tc, t)
    nc = t // tc
    steps = tc // HG_STEP
    unroll = 2 if steps % 2 == 0 else 1
    npair = bw // LANES
    j = jnp.arange(HG_STEP, dtype=jnp.int32)
    tril = (j[:, None] >= j[None, :]).astype(BF16)
    st_spec = pl.BlockSpec((None, npair, LANES, LANES), lambda b, i: (b, 0, 0, 0))
    return pl.pallas_call(
        functools.partial(_hgrn_kernel, steps=steps, unroll=unroll),
        grid=(bsz, nc),
        in_specs=[pl.BlockSpec((tc, w4), lambda b, i: (b * nc + i, 0)),
                  _const_spec((3, bw)), _const_spec((1, bw)),
                  _const_spec((HG_STEP, HG_STEP)), _const_spec((bw, bw)), st_spec],
        out_specs=[pl.BlockSpec((tc, bw), lambda b, i: (b * nc + i, 0)), st_spec],
        out_shape=[jax.ShapeDtypeStruct((m, bw), BF16),
                   jax.ShapeDtypeStruct((bsz, npair, LANES, LANES), F32)],
        scratch_shapes=[pltpu.VMEM((npair, LANES, LANES), F32)],
        compiler_params=_params("parallel", "arbitrary"),
        name="hgrn2",
    )(bproj, lbp, gain, tril, _group_indicator(bw), st0)


def _state_to_tiles(s):
    bsz, h, dk, dv = s.shape
    st = jnp.swapaxes(s, 2, 3).reshape(bsz, h // 2, 2, dv, dk)
    z = jnp.zeros_like(st[:, :, 0])
    top = jnp.concatenate([st[:, :, 0], z], axis=-1)
    bot = jnp.concatenate([z, st[:, :, 1]], axis=-1)
    return jnp.concatenate([top, bot], axis=-2)


def _tiles_to_state(tiles):
    a = tiles[:, :, :HEAD_DIM, :HEAD_DIM]
    b = tiles[:, :, HEAD_DIM:, HEAD_DIM:]
    st = jnp.stack([a, b], axis=2)
    bsz, npair = tiles.shape[:2]
    return jnp.swapaxes(st.reshape(bsz, 2 * npair, HEAD_DIM, HEAD_DIM), 2, 3)


def _sgu_kernel(c_ref, gain_ref, w_ref, bias_ref, ind_ref, y_ref, vn_ref, *, cw):
    c = c_ref[...]
    u = jax.nn.gelu(c[:, :cw])
    gv = jax.nn.gelu(c[:, cw:])
    vn = _group_rmsnorm(gv, ind_ref[...], gain_ref[...])
    vn_ref[...] = vn
    low = _head_masks(c.shape[0])
    mixed = []
    for p in range(cw // LANES):
        vp = vn[:, p * LANES:(p + 1) * LANES].astype(BF16)
        mixed.append(jnp.where(low, _dot(w_ref[2 * p], vp), _dot(w_ref[2 * p + 1], vp)))
    y_ref[...] = (u * (jnp.concatenate(mixed, axis=1) + bias_ref[...])).astype(y_ref.dtype)


def _sgu(cproj, gain, w, bias, *, lc):
    m, w2 = cproj.shape
    cw = w2 // 2
    ng = cw // HEAD_DIM
    row = lambda width: pl.BlockSpec((lc, width), lambda i: (i, 0))
    return pl.pallas_call(
        functools.partial(_sgu_kernel, cw=cw),
        grid=(m // lc,),
        in_specs=[row(w2), _const_spec((1, cw)), _const_spec((ng, lc, lc)),
                  _const_spec((lc, cw)), _const_spec((cw, cw))],
        out_specs=[row(cw), row(cw)],
        out_shape=[jax.ShapeDtypeStruct((m, cw), BF16), jax.ShapeDtypeStruct((m, cw), F32)],
        compiler_params=_params("parallel"),
        name="sgu",
    )(cproj, gain, w, bias, _group_indicator(cw))


def _mix_residual(x_ref, oa_ref, ob_ref, oc_ref, wo_ref):
    aw = oa_ref.shape[1]
    bw = ob_ref.shape[1]
    return (x_ref[...] + _dot(oa_ref[...], wo_ref[0:aw, :])
            + _dot(ob_ref[...], wo_ref[aw:aw + bw, :])
            + _dot(oc_ref[...], wo_ref[aw + bw:, :]))


def _rms(h, g):
    return h * lax.rsqrt(jnp.mean(h * h, axis=-1, keepdims=True) + EPS) * g


def _ffn_dense_kernel(x_ref, oa_ref, ob_ref, oc_ref, wo_ref, g_ref, wg_ref, wu_ref, wd_ref, y_ref,
                      *, fchunk):
    h = _mix_residual(x_ref, oa_ref, ob_ref, oc_ref, wo_ref)
    hn = _rms(h, g_ref[...]).astype(BF16)
    y_ref[...] = h
    for f0 in range(0, wg_ref.shape[1], fchunk):
        cols = slice(f0, f0 + fchunk)
        gate = _dot(hn, wg_ref[:, cols])
        up = _dot(hn, wu_ref[:, cols])
        act = (gate * jax.nn.sigmoid(gate) * up).astype(BF16)
        y_ref[...] += _dot(act, wd_ref[cols, :])


def _ffn_dense(x, oa, ob, oc, wo, g, wg, wu, wd, *, tm):
    m, d = x.shape
    tm = min(tm, m)
    ff = wg.shape[1]
    fchunk = ff // 2 if (ff // 2) % LANES == 0 else ff
    row = lambda a: pl.BlockSpec((tm, a.shape[1]), lambda i: (i, 0))
    return pl.pallas_call(
        functools.partial(_ffn_dense_kernel, fchunk=fchunk),
        grid=(m // tm,),
        in_specs=[row(x), row(oa), row(ob), row(oc), _const_spec(wo.shape), _const_spec((1, d)),
                  _const_spec(wg.shape), _const_spec(wu.shape), _const_spec(wd.shape)],
        out_specs=row(x),
        out_shape=jax.ShapeDtypeStruct((m, d), F32),
        compiler_params=_params("parallel"),
        name="ffn_dense",
    )(x, oa, ob, oc, wo, g.reshape(1, d), wg, wu, wd)


def _route(hn, wr):
    logits = jnp.dot(hn, wr, preferred_element_type=F32, precision=lax.Precision.HIGHEST)
    lane = lax.broadcasted_iota(jnp.int32, logits.shape, 1).astype(F32)
    logits = jnp.where(lane < N_EXPERTS, logits, -jnp.inf)
    m1 = jnp.max(logits, axis=-1, keepdims=True)
    i1 = jnp.min(jnp.where(logits == m1, lane, float(LANES)), axis=-1, keepdims=True)
    rest = jnp.where(lane == i1, -jnp.inf, logits)
    m2 = jnp.max(rest, axis=-1, keepdims=True)
    i2 = jnp.min(jnp.where(rest == m2, lane, float(LANES)), axis=-1, keepdims=True)
    e2 = jnp.exp(m2 - m1)
    g1 = 1.0 / (1.0 + e2)
    g2 = e2 / (1.0 + e2)
    return jnp.where(lane == i1, g1, 0.0) + jnp.where(lane == i2, g2, 0.0)


def _ffn_moe_kernel(x_ref, oa_ref, ob_ref, oc_ref, wo_ref, g_ref, wr_ref, wg_ref, wu_ref, wd_ref,
                    y_ref, h_ref, hn_ref, comb_ref, acc_ref):
    e = pl.program_id(1)

    @pl.when(e == 0)
    def _():
        h = _mix_residual(x_ref, oa_ref, ob_ref, oc_ref, wo_ref)
        hn = _rms(h, g_ref[...])
        h_ref[...] = h
        hn_ref[...] = hn.astype(BF16)
        comb = _route(hn, wr_ref[...])
        for j in range(N_EXPERTS):
            comb_ref[j] = jnp.broadcast_to(comb[:, j:j + 1], comb.shape)
        acc_ref[...] = jnp.zeros_like(acc_ref)

    hn = hn_ref[...]
    gate = _dot(hn, wg_ref[...])
    up = _dot(hn, wu_ref[...])
    act = (gate * jax.nn.sigmoid(gate) * up).astype(BF16)
    out = _dot(act, wd_ref[...])
    cb = comb_ref[e]
    for j in range(out.shape[1] // LANES):
        lanes = slice(j * LANES, (j + 1) * LANES)
        acc_ref[:, lanes] += cb * out[:, lanes]

    @pl.when(e == pl.num_programs(1) - 1)
    def _():
        y_ref[...] = h_ref[...] + acc_ref[...]


def _ffn_moe(x, oa, ob, oc, wo, g, wr, wg, wu, wd, *, tm):
    m, d = x.shape
    tm = min(tm, m)
    ne, _, ff = wg.shape
    row = lambda a: pl.BlockSpec((tm, a.shape[1]), lambda i, e: (i, 0))
    const = lambda shape: pl.BlockSpec(shape, lambda i, e: (0,) * len(shape))
    return pl.pallas_call(
        _ffn_moe_kernel,
        grid=(m // tm, ne),
        in_specs=[row(x), row(oa), row(ob), row(oc), const(wo.shape), const((1, d)),
                  const(wr.shape),
                  pl.BlockSpec((None, d, ff), lambda i, e: (e, 0, 0)),
                  pl.BlockSpec((None, d, ff), lambda i, e: (e, 0, 0)),
                  pl.BlockSpec((None, ff, d), lambda i, e: (e, 0, 0))],
        out_specs=row(x),
        out_shape=jax.ShapeDtypeStruct((m, d), F32),
        scratch_shapes=[pltpu.VMEM((tm, d), F32), pltpu.VMEM((tm, d), BF16),
                        pltpu.VMEM((ne, tm, LANES), F32), pltpu.VMEM((tm, d), F32)],
        compiler_params=_params("parallel", "arbitrary"),
        name="ffn_moe",
    )(x, oa, ob, oc, wo, g.reshape(1, d), wr, wg, wu, wd)


def _tile_gain(g, width):
    return jnp.tile(g.astype(F32), width // g.shape[0]).reshape(1, width)


def kernel(x_prompt, x_sample, cache_k, cache_v, state_b, w_in, w_out, norm_mix, norm_ffn,
           q_norm, k_norm, hgrn_lb, hgrn_out_norm, sgu_norm, sgu_w, sgu_b,
           ffn_w_gate, ffn_w_up, ffn_w_down, moe_router, moe_w_gate, moe_w_up, moe_w_down):
    depth = w_in.shape[0]
    bsz, seq, d = x_prompt.shape
    dbsz, dseq, _ = x_sample.shape
    past = cache_k.shape[2]
    a_heads = cache_k.shape[3]
    b_heads = state_b.shape[2]
    aw = a_heads * HEAD_DIM
    bw = b_heads * HEAD_DIM
    c_groups = sgu_w.shape[1]
    cw = c_groups * HEAD_DIM
    c_chunk = sgu_w.shape[2]

    cs = jnp.cumsum(jax.nn.softmax(hgrn_lb.astype(F32), axis=0), axis=0)
    lb_all = cs - cs[0:1]

    def layer(l, xp, xs):
        w_in_l = w_in[l].astype(BF16)
        w_out_l = w_out[l].astype(BF16)
        qg = _tile_gain(q_norm[l], aw)
        kg = _tile_gain(k_norm[l], aw)
        hg = _tile_gain(hgrn_out_norm[l], bw)
        sg = _tile_gain(sgu_norm[l], cw)
        lb = lb_all[l]
        lbp = jnp.stack([jnp.log(lb), jnp.log1p(-lb), 1.0 - lb])

        def sgu_params(lc):
            tril = jnp.tril(jnp.ones((lc, lc), sgu_w.dtype))
            w = (sgu_w[l][:, :lc, :lc] * tril).astype(BF16)
            bias = jnp.repeat(sgu_b[l][:, :lc].T, HEAD_DIM, axis=1)
            return w, bias

        def ffn(x, oa, ob, oc):
            i = l // 2
            if l % 2 == 0:
                return _ffn_dense(x, oa, ob, oc, w_out_l, norm_ffn[l], ffn_w_gate[i].astype(BF16),
                                  ffn_w_up[i].astype(BF16), ffn_w_down[i].astype(BF16), tm=512)
            wr = jnp.zeros((d, LANES), F32).at[:, :N_EXPERTS].set(moe_router[i])
            return _ffn_moe(x, oa, ob, oc, w_out_l, norm_ffn[l], wr, moe_w_gate[i].astype(BF16),
                            moe_w_up[i].astype(BF16), moe_w_down[i].astype(BF16), tm=512)

        x = xp.reshape(bsz * seq, d)
        q, k, v, kb, vb, bp, cp = _inproj(x, norm_mix[l], w_in_l, qg, kg, aw=aw, bw=bw, cw=cw, tm=512)
        oa = _attn_prompt(q, kb, vb, bsz=bsz, seq=seq)
        st0 = jnp.zeros((bsz, bw // LANES, LANES, LANES), F32)
        ob, st = _hgrn(bp, lbp, hg, st0, bsz=bsz, t=seq, tc=512)
        sw, sb = sgu_params(c_chunk)
        oc, _ = _sgu(cp, sg, sw, sb, lc=c_chunk)
        yp = ffn(x, oa, ob, oc).reshape(bsz, seq, d)
        outs_p = (k.reshape(bsz, seq, a_heads, HEAD_DIM), v.reshape(bsz, seq, a_heads, HEAD_DIM),
                  _tiles_to_state(st))

        x = xs.reshape(dbsz * dseq, d)
        q, k, v, kb, vb, bp, cp = _inproj(x, norm_mix[l], w_in_l, qg, kg, aw=aw, bw=bw, cw=cw, tm=512)
        oa = _attn_sample(q, kb, vb, cache_k[l].reshape(dbsz * past, aw),
                          cache_v[l].reshape(dbsz * past, aw), bsz=dbsz, t=dseq, past=past)
        ob, st = _hgrn(bp, lbp, hg, _state_to_tiles(state_b[l].astype(F32)), bsz=dbsz, t=dseq, tc=dseq)
        sw, sb = sgu_params(dseq)
        oc, vn = _sgu(cp, sg, sw, sb, lc=dseq)
        ys = ffn(x, oa, ob, oc).reshape(dbsz, dseq, d)
        outs_s = (k.reshape(dbsz, dseq, a_heads, HEAD_DIM), v.reshape(dbsz, dseq, a_heads, HEAD_DIM),
                  _tiles_to_state(st), vn.reshape(dbsz, dseq, cw))
        return yp, ys, outs_p, outs_s

    yp, ys = x_prompt, x_sample
    ps, ss = [], []
    for l in range(depth):
        yp, ys, op, os_ = layer(l, yp, ys)
        ps.append(op)
        ss.append(os_)
    stack = lambda items, j: jnp.stack([it[j] for it in items])
    return (yp, ys, stack(ps, 0), stack(ps, 1), stack(ps, 2),
            stack(ss, 0), stack(ss, 1), stack(ss, 2), stack(ss, 3))
```

```python
import functools

import jax
import jax.numpy as jnp
from jax import lax
from jax.experimental import pallas as pl
from jax.experimental.pallas import tpu as pltpu

F32 = jnp.float32
BF16 = jnp.bfloat16

HEAD_DIM = 64
LANES = 128
EPS = 1e-6
N_EXPERTS = 8
TOP_K = 2
VMEM_LIMIT = 56 * 1024 * 1024

ATT_BLOCK = 128
ATT_DEAD = -120.0
HG_STEP = 16
NEG_BIG = -1e30


def _dot(a, b):
    return jnp.dot(a, b, preferred_element_type=F32)


def _dot_nt(a, b):
    return lax.dot_general(a, b, (((1,), (1,)), ((), ())), preferred_element_type=F32)


def _split2(x):
    hi = x.astype(BF16)
    lo = (x - hi.astype(F32)).astype(BF16)
    return hi, lo


def _split3(x):
    hi = x.astype(BF16)
    r = x - hi.astype(F32)
    mid = r.astype(BF16)
    lo = (r - mid.astype(F32)).astype(BF16)
    return hi, mid, lo


def _dot_x2(x, m):
    hi, lo = _split2(x)
    return _dot(hi, m) + _dot(lo, m)


def _ldot_x3(m, x):
    hi, mid, lo = _split3(x)
    return _dot(m, hi) + _dot(m, mid) + _dot(m, lo)


def _log1p_exp_neg_abs(z):
    return jnp.log1p(jnp.exp(-jnp.abs(z)))


def _group_rmsnorm(a, ind, gain_row):
    ms = _dot_x2(a * a, ind) * (1.0 / HEAD_DIM)
    return a * lax.rsqrt(ms + EPS) * gain_row


def _group_indicator(width):
    g = jnp.arange(width, dtype=jnp.int32) // HEAD_DIM
    return (g[:, None] == g[None, :]).astype(BF16)


def _const_spec(shape):
    nd = len(shape)
    return pl.BlockSpec(shape, lambda *_: (0,) * nd)


def _params(*sem):
    return pltpu.CompilerParams(dimension_semantics=sem, vmem_limit_bytes=VMEM_LIMIT)


def _inproj_kernel(x_ref, g_ref, w_ref, qg_ref, kg_ref, ind_ref,
                   q_ref, k_ref, v_ref, kb_ref, vb_ref, b_ref, c_ref, *, aw, bw):
    x = x_ref[...]
    ms = jnp.mean(x * x, axis=-1, keepdims=True)
    xn = (x * lax.rsqrt(ms + EPS) * g_ref[...]).astype(BF16)
    ind = ind_ref[...]
    aq = _dot(xn, w_ref[:, 0:aw])
    q = _group_rmsnorm(aq, ind, qg_ref[...])
    q_ref[...] = (q * (HEAD_DIM ** -0.5)).astype(BF16)
    ak = _dot(xn, w_ref[:, aw:2 * aw])
    k = _group_rmsnorm(ak, ind, kg_ref[...])
    k_ref[...] = k
    kb_ref[...] = k.astype(BF16)
    v = _dot(xn, w_ref[:, 2 * aw:3 * aw])
    v_ref[...] = v
    vb_ref[...] = v.astype(BF16)
    b_ref[...] = _dot(xn, w_ref[:, 3 * aw:3 * aw + 4 * bw])
    c_ref[...] = _dot(xn, w_ref[:, 3 * aw + 4 * bw:])


def _inproj(x, g, w, qg, kg, *, aw, bw, cw, tm):
    m, d = x.shape
    tm = min(tm, m)
    nw = w.shape[1]
    row = lambda width: pl.BlockSpec((tm, width), lambda i: (i, 0))
    return pl.pallas_call(
        functools.partial(_inproj_kernel, aw=aw, bw=bw),
        grid=(m // tm,),
        in_specs=[row(d), _const_spec((1, d)), _const_spec((d, nw)),
                  _const_spec((1, aw)), _const_spec((1, aw)), _const_spec((aw, aw))],
        out_specs=[row(aw), row(aw), row(aw), row(aw), row(aw), row(4 * bw), row(2 * cw)],
        out_shape=[jax.ShapeDtypeStruct((m, aw), BF16),
                   jax.ShapeDtypeStruct((m, aw), F32),
                   jax.ShapeDtypeStruct((m, aw), F32),
                   jax.ShapeDtypeStruct((m, aw), BF16),
                   jax.ShapeDtypeStruct((m, aw), BF16),
                   jax.ShapeDtypeStruct((m, 4 * bw), F32),
                   jax.ShapeDtypeStruct((m, 2 * cw), F32)],
        compiler_params=_params("parallel"),
        name="inproj",
    )(x, g.reshape(1, d), w, qg, kg, _group_indicator(aw))


def _head_masks(rows):
    lane = lax.broadcasted_iota(jnp.int32, (rows, LANES), 1)
    return lane < HEAD_DIM


def _sb_block(qs, kblk, vblk, carry_ref, acc_ref, cum2, low, mask):
    tk = kblk.shape[0]
    rows = qs[0].shape[0] // 2
    pair = lambda p: slice(p * LANES, (p + 1) * LANES)
    z = jnp.concatenate([_dot_nt(q, kblk[:, pair(p)]) for p, q in enumerate(qs)], axis=0)
    n = z.shape[0]
    sp = jnp.log(1.0 + jnp.exp(-jnp.abs(z)))
    log_beta = jnp.minimum(z, 0.0) - sp
    log_keep = log_beta - z
    if mask is not None:
        log_keep = jnp.where(mask, log_keep, 0.0)
    hi, lo = _split2(log_keep)
    r = _dot(jnp.concatenate([hi, lo], axis=0), cum2)
    r = r[:n] + r[n:]
    carry = carry_ref[...]
    after = r[:, LANES:] + (carry if tk == LANES else carry[:, :tk])
    w = jnp.exp(log_beta + after)
    if mask is not None:
        w = jnp.where(mask, w, 0.0)
    w = w.astype(BF16)
    for p in range(len(qs)):
        pv = _dot(w[2 * p * rows:(2 * p + 2) * rows], vblk[:, pair(p)])
        acc_ref[:, pair(p)] += jnp.where(low, pv[:rows], pv[rows:])
    carry_ref[...] = carry + r[:, :LANES]


def _stack_heads(q, low):
    qs = []
    for p in range(q.shape[1] // LANES):
        qp = q[:, p * LANES:(p + 1) * LANES]
        zero = jnp.zeros_like(qp)
        qs.append(jnp.concatenate([jnp.where(low, qp, zero), jnp.where(low, zero, qp)], axis=0))
    return qs


def _causal_mask(rows, tk, heads):
    r = lax.broadcasted_iota(jnp.int32, (heads * rows, tk), 0) & (rows - 1)
    c = lax.broadcasted_iota(jnp.int32, (heads * rows, tk), 1)
    return c < r


def _attn_prompt_kernel(q_ref, k_ref, v_ref, cum_ref, o_ref, carry_ref, acc_ref, *, blk, heads):
    i = pl.program_id(1)
    low = _head_masks(blk)
    qs = _stack_heads(q_ref[...], low)
    carry_ref[...] = jnp.zeros_like(carry_ref)
    acc_ref[...] = jnp.zeros_like(acc_ref)
    cum2 = cum_ref[...]
    start = pl.multiple_of(i * blk, blk)
    _sb_block(qs, k_ref[pl.ds(start, blk), :], v_ref[pl.ds(start, blk), :],
              carry_ref, acc_ref, cum2, low, _causal_mask(blk, blk, heads))

    def cond(state):
        kb, live = state
        return jnp.logical_and(kb >= 0, live > ATT_DEAD)

    def body(state):
        kb, _ = state
        s = pl.multiple_of(kb * blk, blk)
        _sb_block(qs, k_ref[pl.ds(s, blk), :], v_ref[pl.ds(s, blk), :],
                  carry_ref, acc_ref, cum2, low, None)
        return kb - 1, jnp.max(carry_ref[...])

    lax.while_loop(cond, body, (i - 1, jnp.max(carry_ref[...])))
    o_ref[...] = acc_ref[...].astype(o_ref.dtype)


def _cum_matrix(n):
    j = jnp.arange(n, dtype=jnp.int32)
    newer = (j[:, None] > j[None, :]).astype(BF16)
    return jnp.concatenate([jnp.ones((n, LANES), BF16), newer], axis=1)


def _attn_prompt(q, kb, vb, *, bsz, seq):
    m, aw = q.shape
    blk = ATT_BLOCK
    nq = seq // blk
    nh = aw // HEAD_DIM
    return pl.pallas_call(
        functools.partial(_attn_prompt_kernel, blk=blk, heads=nh),
        grid=(bsz, nq),
        in_specs=[pl.BlockSpec((blk, aw), lambda b, i: (b * nq + i, 0)),
                  pl.BlockSpec((seq, aw), lambda b, i: (b, 0)),
                  pl.BlockSpec((seq, aw), lambda b, i: (b, 0)),
                  _const_spec((blk, LANES + blk))],
        out_specs=pl.BlockSpec((blk, aw), lambda b, i: (b * nq + i, 0)),
        out_shape=jax.ShapeDtypeStruct((m, aw), BF16),
        scratch_shapes=[pltpu.VMEM((nh * blk, LANES), F32), pltpu.VMEM((blk, aw), F32)],
        compiler_params=_params("parallel", "arbitrary"),
        name="attn_prompt",
    )(q, kb, vb, _cum_matrix(blk))


def _attn_sample_kernel(q_ref, kn_ref, vn_ref, ck_ref, cv_ref, cum_new_ref, cum_ref,
                        o_ref, carry_ref, acc_ref, *, t, blk, past, heads):
    low = _head_masks(t)
    qs = _stack_heads(q_ref[...], low)
    carry_ref[...] = jnp.zeros_like(carry_ref)
    acc_ref[...] = jnp.zeros_like(acc_ref)
    _sb_block(qs, kn_ref[...], vn_ref[...], carry_ref, acc_ref,
              cum_new_ref[...], low, _causal_mask(t, t, heads))
    cum2 = cum_ref[...]

    def cond(state):
        kb, live = state
        return jnp.logical_and(kb >= 0, live > ATT_DEAD)

    def body(state):
        kb, _ = state
        s = pl.multiple_of(kb * blk, blk)
        _sb_block(qs, ck_ref[pl.ds(s, blk), :].astype(BF16), cv_ref[pl.ds(s, blk), :].astype(BF16),
                  carry_ref, acc_ref, cum2, low, None)
        return kb - 1, jnp.max(carry_ref[...])

    lax.while_loop(cond, body, (past // blk - 1, jnp.max(carry_ref[...])))
    o_ref[...] = acc_ref[...].astype(o_ref.dtype)


def _attn_sample(q, kb, vb, ck, cv, *, bsz, t, past):
    m, aw = q.shape
    blk = ATT_BLOCK
    nh = aw // HEAD_DIM
    new = pl.BlockSpec((t, aw), lambda b: (b, 0))
    old = pl.BlockSpec((past, aw), lambda b: (b, 0))
    return pl.pallas_call(
        functools.partial(_attn_sample_kernel, t=t, blk=blk, past=past, heads=nh),
        grid=(bsz,),
        in_specs=[new, new, new, old, old,
                  _const_spec((t, LANES + t)), _const_spec((blk, LANES + blk))],
        out_specs=new,
        out_shape=jax.ShapeDtypeStruct((m, aw), BF16),
        scratch_shapes=[pltpu.VMEM((nh * t, LANES), F32), pltpu.VMEM((t, aw), F32)],
        compiler_params=_params("parallel"),
        name="attn_sample",
    )(q, kb, vb, ck, cv, _cum_matrix(t), _cum_matrix(blk))


def _hgrn_group(blk, lbp, gain, tril, ind, bd_mask, st_ref):
    bw = gain.shape[1]
    n = HG_STEP
    half = n // 2
    steps = blk.shape[0] // n
    npair = bw // LANES
    pair = lambda p: slice(p * LANES, (p + 1) * LANES)
    qr = blk[:, 0:bw]
    z = blk[:, bw:2 * bw]
    v = blk[:, 2 * bw:3 * bw]
    gr = blk[:, 3 * bw:4 * bw]
    log_lb, log1m_lb, one_m_lb = lbp[0:1], lbp[1:2], lbp[2:3]

    sp = _log1p_exp_neg_abs(z)
    cc = log1m_lb + (jnp.minimum(z, 0.0) - sp)
    log_f = jnp.maximum(log_lb, cc) + jnp.log1p(jnp.exp(-jnp.abs(log_lb - cc)))
    k = one_m_lb * jnp.exp(jnp.minimum(-z, 0.0) - sp)
    q = qr * jax.nn.sigmoid(qr)
    b = _ldot_x3(tril, log_f)
    vb = v.astype(BF16)

    row = lax.broadcasted_iota(jnp.int32, (n, bw), 0)
    row_hi = lax.broadcasted_iota(jnp.int32, (half, bw), 0) + half
    parts, qbs, decays, upds = [], [], [], []
    for j in range(steps):
        sl = slice(j * n, (j + 1) * n)
        bj, qj, kj = b[sl], q[sl], k[sl]
        b_last = bj[n - 1:n]
        qbs.append((qj * jnp.exp(bj)).astype(BF16))
        kd = (kj * jnp.exp(b_last - bj)).astype(BF16)
        decays.append(jnp.exp(b_last))
        upds.append([lax.dot_general(vb[sl, pair(p)], kd[:, pair(p)], (((0,), (0,)), ((), ())),
                                     preferred_element_type=F32) for p in range(npair)])
        for s in range(half):
            e = jnp.exp(jnp.where(row >= s, bj - bj[s:s + 1], NEG_BIG))
            parts.append((qj * e) * kj[s:s + 1])
        bh, qh = bj[half:], qj[half:]
        for s in range(half, n):
            e = jnp.exp(jnp.where(row_hi >= s, bh - bj[s:s + 1], NEG_BIG))
            parts.append((qh * e) * kj[s:s + 1])
    att = _dot(jnp.concatenate(parts, axis=0).astype(BF16), ind)
    per_step = half * n + half * half

    sts = [st_ref[p] for p in range(npair)]
    outs = []
    for j in range(steps):
        base = j * per_step
        vj = v[j * n:(j + 1) * n]
        o_lo = att[base:base + n] * vj[0:1]
        for s in range(1, half):
            o_lo = o_lo + att[base + s * n:base + (s + 1) * n] * vj[s:s + 1]
        base += half * n
        o_hi = att[base:base + half] * vj[half:half + 1]
        for s in range(half + 1, n):
            o_hi = o_hi + att[base + (s - half) * half:base + (s - half + 1) * half] * vj[s:s + 1]
        o_intra = jnp.concatenate([o_lo[:half], o_lo[half:] + o_hi], axis=0)
        o_inter = []
        for p in range(npair):
            o_inter.append(_dot_nt(qbs[j][:, pair(p)], sts[p].astype(BF16)))
            sts[p] = sts[p] * decays[j][:, pair(p)] + jnp.where(bd_mask, upds[j][p], 0.0)
        outs.append(o_intra + jnp.concatenate(o_inter, axis=1))
    for p in range(npair):
        st_ref[p] = sts[p]
    o = jnp.concatenate(outs, axis=0) if steps > 1 else outs[0]

    ms = _dot_x2(o * o, ind) * (1.0 / HEAD_DIM)
    o = o * lax.rsqrt(ms + EPS) * gain
    return o * (gr * jax.nn.sigmoid(gr))


def _hgrn_kernel(b_ref, lbp_ref, gain_ref, tril_ref, ind_ref, st0_ref, o_ref, st_out_ref, st_ref,
                 *, groups, grows):
    i = pl.program_id(1)

    @pl.when(i == 0)
    def _():
        st_ref[...] = st0_ref[...]

    lbp = lbp_ref[...]
    gain = gain_ref[...]
    tril = tril_ref[...]
    ind = ind_ref[...]
    r = lax.broadcasted_iota(jnp.int32, (LANES, LANES), 0) // HEAD_DIM
    c = lax.broadcasted_iota(jnp.int32, (LANES, LANES), 1) // HEAD_DIM
    bd_mask = r == c

    def body(j, _):
        rows = pl.ds(pl.multiple_of(j * grows, grows), grows)
        o = _hgrn_group(b_ref[rows, :], lbp, gain, tril, ind, bd_mask, st_ref)
        o_ref[rows, :] = o.astype(o_ref.dtype)
        return 0

    lax.fori_loop(0, groups, body, 0)

    @pl.when(i == pl.num_programs(1) - 1)
    def _():
        st_out_ref[...] = st_ref[...]


def _hgrn(bproj, lbp, gain, st0, *, bsz, t, tc, gsteps=4):
    m, w4 = bproj.shape
    bw = w4 // 4
    tc = min(tc, t)
    nc = t // tc
    grows = min(gsteps * HG_STEP, tc)
    npair = bw // LANES
    j = jnp.arange(grows, dtype=jnp.int32)
    tril = jnp.logical_and(j[:, None] >= j[None, :],
                           j[:, None] // HG_STEP == j[None, :] // HG_STEP).astype(BF16)
    st_spec = pl.BlockSpec((None, npair, LANES, LANES), lambda b, i: (b, 0, 0, 0))
    return pl.pallas_call(
        functools.partial(_hgrn_kernel, groups=tc // grows, grows=grows),
        grid=(bsz, nc),
        in_specs=[pl.BlockSpec((tc, w4), lambda b, i: (b * nc + i, 0)),
                  _const_spec((3, bw)), _const_spec((1, bw)),
                  _const_spec((grows, grows)), _const_spec((bw, bw)), st_spec],
        out_specs=[pl.BlockSpec((tc, bw), lambda b, i: (b * nc + i, 0)), st_spec],
        out_shape=[jax.ShapeDtypeStruct((m, bw), BF16),
                   jax.ShapeDtypeStruct((bsz, npair, LANES, LANES), F32)],
        scratch_shapes=[pltpu.VMEM((npair, LANES, LANES), F32)],
        compiler_params=_params("parallel", "arbitrary"),
        name="hgrn2",
    )(bproj, lbp, gain, tril, _group_indicator(bw), st0)


def _state_to_tiles(s):
    bsz, h, dk, dv = s.shape
    st = jnp.swapaxes(s, 2, 3).reshape(bsz, h // 2, 2, dv, dk)
    z = jnp.zeros_like(st[:, :, 0])
    top = jnp.concatenate([st[:, :, 0], z], axis=-1)
    bot = jnp.concatenate([z, st[:, :, 1]], axis=-1)
    return jnp.concatenate([top, bot], axis=-2)


def _tiles_to_state(tiles):
    a = tiles[:, :, :HEAD_DIM, :HEAD_DIM]
    b = tiles[:, :, HEAD_DIM:, HEAD_DIM:]
    st = jnp.stack([a, b], axis=2)
    bsz, npair = tiles.shape[:2]
    return jnp.swapaxes(st.reshape(bsz, 2 * npair, HEAD_DIM, HEAD_DIM), 2, 3)


def _sgu_kernel(c_ref, gain_ref, w_ref, bias_ref, ind_ref, y_ref, vn_ref, *, cw):
    c = c_ref[...]
    u = jax.nn.gelu(c[:, :cw])
    gv = jax.nn.gelu(c[:, cw:])
    vn = _group_rmsnorm(gv, ind_ref[...], gain_ref[...])
    vn_ref[...] = vn
    low = _head_masks(c.shape[0])
    mixed = []
    for p in range(cw // LANES):
        vp = vn[:, p * LANES:(p + 1) * LANES].astype(BF16)
        mixed.append(jnp.where(low, _dot(w_ref[2 * p], vp), _dot(w_ref[2 * p + 1], vp)))
    y_ref[...] = (u * (jnp.concatenate(mixed, axis=1) + bias_ref[...])).astype(y_ref.dtype)


def _sgu(cproj, gain, w, bias, *, lc):
    m, w2 = cproj.shape
    cw = w2 // 2
    ng = cw // HEAD_DIM
    row = lambda width: pl.BlockSpec((lc, width), lambda i: (i, 0))
    return pl.pallas_call(
        functools.partial(_sgu_kernel, cw=cw),
        grid=(m // lc,),
        in_specs=[row(w2), _const_spec((1, cw)), _const_spec((ng, lc, lc)),
                  _const_spec((lc, cw)), _const_spec((cw, cw))],
        out_specs=[row(cw), row(cw)],
        out_shape=[jax.ShapeDtypeStruct((m, cw), BF16), jax.ShapeDtypeStruct((m, cw), F32)],
        compiler_params=_params("parallel"),
        name="sgu",
    )(cproj, gain, w, bias, _group_indicator(cw))


def _mix_residual(x_ref, oa_ref, ob_ref, oc_ref, wo_ref):
    aw = oa_ref.shape[1]
    bw = ob_ref.shape[1]
    return (x_ref[...] + _dot(oa_ref[...], wo_ref[0:aw, :])
            + _dot(ob_ref[...], wo_ref[aw:aw + bw, :])
            + _dot(oc_ref[...], wo_ref[aw + bw:, :]))


def _rms(h, g):
    return h * lax.rsqrt(jnp.mean(h * h, axis=-1, keepdims=True) + EPS) * g


def _ffn_dense_kernel(x_ref, oa_ref, ob_ref, oc_ref, wo_ref, g_ref, wg_ref, wu_ref, wd_ref, y_ref,
                      *, fchunk):
    h = _mix_residual(x_ref, oa_ref, ob_ref, oc_ref, wo_ref)
    hn = _rms(h, g_ref[...]).astype(BF16)
    y_ref[...] = h
    for f0 in range(0, wg_ref.shape[1], fchunk):
        cols = slice(f0, f0 + fchunk)
        gate = _dot(hn, wg_ref[:, cols])
        up = _dot(hn, wu_ref[:, cols])
        act = (gate * jax.nn.sigmoid(gate) * up).astype(BF16)
        y_ref[...] += _dot(act, wd_ref[cols, :])


def _ffn_dense(x, oa, ob, oc, wo, g, wg, wu, wd, *, tm):
    m, d = x.shape
    tm = min(tm, m)
    ff = wg.shape[1]
    fchunk = ff // 2 if (ff // 2) % LANES == 0 else ff
    row = lambda a: pl.BlockSpec((tm, a.shape[1]), lambda i: (i, 0))
    return pl.pallas_call(
        functools.partial(_ffn_dense_kernel, fchunk=fchunk),
        grid=(m // tm,),
        in_specs=[row(x), row(oa), row(ob), row(oc), _const_spec(wo.shape), _const_spec((1, d)),
                  _const_spec(wg.shape), _const_spec(wu.shape), _const_spec(wd.shape)],
        out_specs=row(x),
        out_shape=jax.ShapeDtypeStruct((m, d), F32),
        compiler_params=_params("parallel"),
        name="ffn_dense",
    )(x, oa, ob, oc, wo, g.reshape(1, d), wg, wu, wd)


def _route(hn, wr):
    logits = jnp.dot(hn, wr, preferred_element_type=F32, precision=lax.Precision.HIGHEST)
    lane = lax.broadcasted_iota(jnp.int32, logits.shape, 1).astype(F32)
    logits = jnp.where(lane < N_EXPERTS, logits, -jnp.inf)
    m1 = jnp.max(logits, axis=-1, keepdims=True)
    i1 = jnp.min(jnp.where(logits == m1, lane, float(LANES)), axis=-1, keepdims=True)
    rest = jnp.where(lane == i1, -jnp.inf, logits)
    m2 = jnp.max(rest, axis=-1, keepdims=True)
    i2 = jnp.min(jnp.where(rest == m2, lane, float(LANES)), axis=-1, keepdims=True)
    e2 = jnp.exp(m2 - m1)
    g1 = 1.0 / (1.0 + e2)
    g2 = e2 / (1.0 + e2)
    return jnp.where(lane == i1, g1, 0.0) + jnp.where(lane == i2, g2, 0.0)


def _ffn_moe_kernel(x_ref, oa_ref, ob_ref, oc_ref, wo_ref, g_ref, wr_ref, wg_ref, wu_ref, wd_ref,
                    y_ref, h_ref, hn_ref, comb_ref, acc_ref):
    e = pl.program_id(1)

    @pl.when(e == 0)
    def _():
        h = _mix_residual(x_ref, oa_ref, ob_ref, oc_ref, wo_ref)
        hn = _rms(h, g_ref[...])
        h_ref[...] = h
        hn_ref[...] = hn.astype(BF16)
        comb = _route(hn, wr_ref[...])
        for j in range(N_EXPERTS):
            comb_ref[j] = jnp.broadcast_to(comb[:, j:j + 1], comb.shape)
        acc_ref[...] = jnp.zeros_like(acc_ref)

    hn = hn_ref[...]
    gate = _dot(hn, wg_ref[...])
    up = _dot(hn, wu_ref[...])
    act = (gate * jax.nn.sigmoid(gate) * up).astype(BF16)
    out = _dot(act, wd_ref[...])
    cb = comb_ref[e]
    for j in range(out.shape[1] // LANES):
        lanes = slice(j * LANES, (j + 1) * LANES)
        acc_ref[:, lanes] += cb * out[:, lanes]

    @pl.when(e == pl.num_programs(1) - 1)
    def _():
        y_ref[...] = h_ref[...] + acc_ref[...]


def _ffn_moe(x, oa, ob, oc, wo, g, wr, wg, wu, wd, *, tm):
    m, d = x.shape
    tm = min(tm, m)
    ne, _, ff = wg.shape
    row = lambda a: pl.BlockSpec((tm, a.shape[1]), lambda i, e: (i, 0))
    const = lambda shape: pl.BlockSpec(shape, lambda i, e: (0,) * len(shape))
    return pl.pallas_call(
        _ffn_moe_kernel,
        grid=(m // tm, ne),
        in_specs=[row(x), row(oa), row(ob), row(oc), const(wo.shape), const((1, d)),
                  const(wr.shape),
                  pl.BlockSpec((None, d, ff), lambda i, e: (e, 0, 0)),
                  pl.BlockSpec((None, d, ff), lambda i, e: (e, 0, 0)),
                  pl.BlockSpec((None, ff, d), lambda i, e: (e, 0, 0))],
        out_specs=row(x),
        out_shape=jax.ShapeDtypeStruct((m, d), F32),
        scratch_shapes=[pltpu.VMEM((tm, d), F32), pltpu.VMEM((tm, d), BF16),
                        pltpu.VMEM((ne, tm, LANES), F32), pltpu.VMEM((tm, d), F32)],
        compiler_params=_params("parallel", "arbitrary"),
        name="ffn_moe",
    )(x, oa, ob, oc, wo, g.reshape(1, d), wr, wg, wu, wd)


def _tile_gain(g, width):
    return jnp.tile(g.astype(F32), width // g.shape[0]).reshape(1, width)


def kernel(x_prompt, x_sample, cache_k, cache_v, state_b, w_in, w_out, norm_mix, norm_ffn,
           q_norm, k_norm, hgrn_lb, hgrn_out_norm, sgu_norm, sgu_w, sgu_b,
           ffn_w_gate, ffn_w_up, ffn_w_down, moe_router, moe_w_gate, moe_w_up, moe_w_down):
    depth = w_in.shape[0]
    bsz, seq, d = x_prompt.shape
    dbsz, dseq, _ = x_sample.shape
    past = cache_k.shape[2]
    a_heads = cache_k.shape[3]
    b_heads = state_b.shape[2]
    aw = a_heads * HEAD_DIM
    bw = b_heads * HEAD_DIM
    c_groups = sgu_w.shape[1]
    cw = c_groups * HEAD_DIM
    c_chunk = sgu_w.shape[2]

    cs = jnp.cumsum(jax.nn.softmax(hgrn_lb.astype(F32), axis=0), axis=0)
    lb_all = cs - cs[0:1]

    def layer(l, xp, xs):
        w_in_l = w_in[l].astype(BF16)
        w_out_l = w_out[l].astype(BF16)
        qg = _tile_gain(q_norm[l], aw)
        kg = _tile_gain(k_norm[l], aw)
        hg = _tile_gain(hgrn_out_norm[l], bw)
        sg = _tile_gain(sgu_norm[l], cw)
        lb = lb_all[l]
        lbp = jnp.stack([jnp.log(lb), jnp.log1p(-lb), 1.0 - lb])

        def sgu_params(lc):
            tril = jnp.tril(jnp.ones((lc, lc), sgu_w.dtype))
            w = (sgu_w[l][:, :lc, :lc] * tril).astype(BF16)
            bias = jnp.repeat(sgu_b[l][:, :lc].T, HEAD_DIM, axis=1)
            return w, bias

        def ffn(x, oa, ob, oc):
            i = l // 2
            if l % 2 == 0:
                return _ffn_dense(x, oa, ob, oc, w_out_l, norm_ffn[l], ffn_w_gate[i].astype(BF16),
                                  ffn_w_up[i].astype(BF16), ffn_w_down[i].astype(BF16), tm=512)
            wr = jnp.zeros((d, LANES), F32).at[:, :N_EXPERTS].set(moe_router[i])
            return _ffn_moe(x, oa, ob, oc, w_out_l, norm_ffn[l], wr, moe_w_gate[i].astype(BF16),
                            moe_w_up[i].astype(BF16), moe_w_down[i].astype(BF16), tm=512)

        x = xp.reshape(bsz * seq, d)
        q, k, v, kb, vb, bp, cp = _inproj(x, norm_mix[l], w_in_l, qg, kg, aw=aw, bw=bw, cw=cw, tm=512)
        oa = _attn_prompt(q, kb, vb, bsz=bsz, seq=seq)
        st0 = jnp.zeros((bsz, bw // LANES, LANES, LANES), F32)
        ob, st = _hgrn(bp, lbp, hg, st0, bsz=bsz, t=seq, tc=512)
        sw, sb = sgu_params(c_chunk)
        oc, _ = _sgu(cp, sg, sw, sb, lc=c_chunk)
        yp = ffn(x, oa, ob, oc).reshape(bsz, seq, d)
        outs_p = (k.reshape(bsz, seq, a_heads, HEAD_DIM), v.reshape(bsz, seq, a_heads, HEAD_DIM),
                  _tiles_to_state(st))

        x = xs.reshape(dbsz * dseq, d)
        q, k, v, kb, vb, bp, cp = _inproj(x, norm_mix[l], w_in_l, qg, kg, aw=aw, bw=bw, cw=cw, tm=512)
        oa = _attn_sample(q, kb, vb, cache_k[l].reshape(dbsz * past, aw),
                          cache_v[l].reshape(dbsz * past, aw), bsz=dbsz, t=dseq, past=past)
        ob, st = _hgrn(bp, lbp, hg, _state_to_tiles(state_b[l].astype(F32)), bsz=dbsz, t=dseq, tc=dseq)
        sw, sb = sgu_params(dseq)
        oc, vn = _sgu(cp, sg, sw, sb, lc=dseq)
        ys = ffn(x, oa, ob, oc).reshape(dbsz, dseq, d)
        outs_s = (k.reshape(dbsz, dseq, a_heads, HEAD_DIM), v.reshape(dbsz, dseq, a_heads, HEAD_DIM),
                  _tiles_to_state(st), vn.reshape(dbsz, dseq, cw))
        return yp, ys, outs_p, outs_s

    yp, ys = x_prompt, x_sample
    ps, ss = [], []
    for l in range(depth):
        yp, ys, op, os_ = layer(l, yp, ys)
        ps.append(op)
        ss.append(os_)
    stack = lambda items, j: jnp.stack([it[j] for it in items])
    return (yp, ys, stack(ps, 0), stack(ps, 1), stack(ps, 2),
            stack(ss, 0), stack(ss, 1), stack(ss, 2), stack(ss, 3))
```

```python
import functools

import jax
import jax.numpy as jnp
from jax import lax
from jax.experimental import pallas as pl
from jax.experimental.pallas import tpu as pltpu

F32 = jnp.float32
BF16 = jnp.bfloat16

HEAD_DIM = 64
LANES = 128
EPS = 1e-6
N_EXPERTS = 8
TOP_K = 2
VMEM_LIMIT = 56 * 1024 * 1024

ATT_BLOCK = 128
ATT_DEAD = -120.0
HG_STEP = 16
NEG_BIG = -1e30


def _dot(a, b):
    return jnp.dot(a, b, preferred_element_type=F32)


def _dot_nt(a, b):
    return lax.dot_general(a, b, (((1,), (1,)), ((), ())), preferred_element_type=F32)


def _split2(x):
    hi = x.astype(BF16)
    lo = (x - hi.astype(F32)).astype(BF16)
    return hi, lo


def _split3(x):
    hi = x.astype(BF16)
    r = x - hi.astype(F32)
    mid = r.astype(BF16)
    lo = (r - mid.astype(F32)).astype(BF16)
    return hi, mid, lo


def _dot_x2(x, m):
    hi, lo = _split2(x)
    return _dot(hi, m) + _dot(lo, m)


def _ldot_x3(m, x):
    hi, mid, lo = _split3(x)
    return _dot(m, hi) + _dot(m, mid) + _dot(m, lo)


def _log1p_exp_neg_abs(z):
    return jnp.log1p(jnp.exp(-jnp.abs(z)))


def _group_rmsnorm(a, ind, gain_row):
    ms = _dot_x2(a * a, ind) * (1.0 / HEAD_DIM)
    return a * lax.rsqrt(ms + EPS) * gain_row


def _group_indicator(width):
    g = jnp.arange(width, dtype=jnp.int32) // HEAD_DIM
    return (g[:, None] == g[None, :]).astype(BF16)


def _const_spec(shape):
    nd = len(shape)
    return pl.BlockSpec(shape, lambda *_: (0,) * nd)


def _params(*sem):
    return pltpu.CompilerParams(dimension_semantics=sem, vmem_limit_bytes=VMEM_LIMIT)


def _inproj_kernel(x_ref, g_ref, w_ref, qg_ref, kg_ref, ind_ref,
                   q_ref, k_ref, v_ref, kb_ref, vb_ref, b_ref, c_ref, *, aw, bw):
    x = x_ref[...]
    ms = jnp.mean(x * x, axis=-1, keepdims=True)
    xn = (x * lax.rsqrt(ms + EPS) * g_ref[...]).astype(BF16)
    ind = ind_ref[...]
    aq = _dot(xn, w_ref[:, 0:aw])
    q = _group_rmsnorm(aq, ind, qg_ref[...])
    q_ref[...] = (q * (HEAD_DIM ** -0.5)).astype(BF16)
    ak = _dot(xn, w_ref[:, aw:2 * aw])
    k = _group_rmsnorm(ak, ind, kg_ref[...])
    kb_ref[...] = k.astype(BF16)
    v = _dot(xn, w_ref[:, 2 * aw:3 * aw])
    vb_ref[...] = v.astype(BF16)
    for h in range(aw // HEAD_DIM):
        k_ref[:, h, :] = k[:, h * HEAD_DIM:(h + 1) * HEAD_DIM]
        v_ref[:, h, :] = v[:, h * HEAD_DIM:(h + 1) * HEAD_DIM]
    b_ref[...] = _dot(xn, w_ref[:, 3 * aw:3 * aw + 4 * bw])
    c_ref[...] = _dot(xn, w_ref[:, 3 * aw + 4 * bw:])


def _inproj(x, g, w, qg, kg, *, aw, bw, cw, tm):
    m, d = x.shape
    tm = min(tm, m)
    nw = w.shape[1]
    row = lambda width: pl.BlockSpec((tm, width), lambda i: (i, 0))
    nh = aw // HEAD_DIM
    heads = pl.BlockSpec((tm, nh, HEAD_DIM), lambda i: (i, 0, 0))
    return pl.pallas_call(
        functools.partial(_inproj_kernel, aw=aw, bw=bw),
        grid=(m // tm,),
        in_specs=[row(d), _const_spec((1, d)), _const_spec((d, nw)),
                  _const_spec((1, aw)), _const_spec((1, aw)), _const_spec((aw, aw))],
        out_specs=[row(aw), heads, heads, row(aw), row(aw), row(4 * bw), row(2 * cw)],
        out_shape=[jax.ShapeDtypeStruct((m, aw), BF16),
                   jax.ShapeDtypeStruct((m, nh, HEAD_DIM), F32),
                   jax.ShapeDtypeStruct((m, nh, HEAD_DIM), F32),
                   jax.ShapeDtypeStruct((m, aw), BF16),
                   jax.ShapeDtypeStruct((m, aw), BF16),
                   jax.ShapeDtypeStruct((m, 4 * bw), F32),
                   jax.ShapeDtypeStruct((m, 2 * cw), F32)],
        compiler_params=_params("parallel"),
        name="inproj",
    )(x, g.reshape(1, d), w, qg, kg, _group_indicator(aw))


def _head_masks(rows):
    lane = lax.broadcasted_iota(jnp.int32, (rows, LANES), 1)
    return lane < HEAD_DIM


def _sb_block(qs, kblk, vblk, carry_ref, acc_ref, cum2, low, mask):
    tk = kblk.shape[0]
    rows = qs[0].shape[0] // 2
    n = 2 * rows
    npair = len(qs)
    pair = lambda p: slice(p * LANES, (p + 1) * LANES)
    zs = [_dot_nt(q, kblk[:, pair(p)]) for p, q in enumerate(qs)]
    log_betas, rs = [], []
    for p in range(npair):
        z = zs[p]
        sp = jnp.log(1.0 + jnp.exp(-jnp.abs(z)))
        log_beta = jnp.minimum(z, 0.0) - sp
        log_keep = log_beta - z
        if mask is not None:
            log_keep = jnp.where(mask, log_keep, 0.0)
        hi, lo = _split2(log_keep)
        r = _dot(jnp.concatenate([hi, lo], axis=0), cum2)
        log_betas.append(log_beta)
        rs.append(r[:n] + r[n:])
    for p in range(npair):
        carry = carry_ref[p * n:(p + 1) * n, :]
        after = rs[p][:, LANES:] + (carry if tk == LANES else carry[:, :tk])
        w = jnp.exp(log_betas[p] + after)
        if mask is not None:
            w = jnp.where(mask, w, 0.0)
        pv = _dot(w.astype(BF16), vblk[:, pair(p)])
        acc_ref[:, pair(p)] += jnp.where(low, pv[:rows], pv[rows:])
        carry_ref[p * n:(p + 1) * n, :] = carry + rs[p][:, :LANES]


def _stack_heads(q, low):
    qs = []
    for p in range(q.shape[1] // LANES):
        qp = q[:, p * LANES:(p + 1) * LANES]
        zero = jnp.zeros_like(qp)
        qs.append(jnp.concatenate([jnp.where(low, qp, zero), jnp.where(low, zero, qp)], axis=0))
    return qs


def _causal_mask(rows, tk, heads):
    r = lax.broadcasted_iota(jnp.int32, (heads * rows, tk), 0) & (rows - 1)
    c = lax.broadcasted_iota(jnp.int32, (heads * rows, tk), 1)
    return c < r


def _attn_prompt_kernel(q_ref, k_ref, v_ref, cum_ref, o_ref, carry_ref, acc_ref, *, blk, heads):
    i = pl.program_id(1)
    low = _head_masks(blk)
    qs = _stack_heads(q_ref[...], low)
    carry_ref[...] = jnp.zeros_like(carry_ref)
    acc_ref[...] = jnp.zeros_like(acc_ref)
    cum2 = cum_ref[...]
    start = pl.multiple_of(i * blk, blk)
    _sb_block(qs, k_ref[pl.ds(start, blk), :], v_ref[pl.ds(start, blk), :],
              carry_ref, acc_ref, cum2, low, _causal_mask(blk, blk, 2))

    def cond(state):
        kb, live = state
        return jnp.logical_and(kb >= 0, live > ATT_DEAD)

    def body(state):
        kb, _ = state
        s = pl.multiple_of(kb * blk, blk)
        _sb_block(qs, k_ref[pl.ds(s, blk), :], v_ref[pl.ds(s, blk), :],
                  carry_ref, acc_ref, cum2, low, None)
        return kb - 1, jnp.max(carry_ref[...])

    lax.while_loop(cond, body, (i - 1, jnp.max(carry_ref[...])))
    o_ref[...] = acc_ref[...].astype(o_ref.dtype)


def _cum_matrix(n):
    j = jnp.arange(n, dtype=jnp.int32)
    newer = (j[:, None] > j[None, :]).astype(BF16)
    return jnp.concatenate([jnp.ones((n, LANES), BF16), newer], axis=1)


def _attn_prompt(q, kb, vb, *, bsz, seq):
    m, aw = q.shape
    blk = ATT_BLOCK
    nq = seq // blk
    nh = aw // HEAD_DIM
    return pl.pallas_call(
        functools.partial(_attn_prompt_kernel, blk=blk, heads=nh),
        grid=(bsz, nq),
        in_specs=[pl.BlockSpec((blk, aw), lambda b, i: (b * nq + i, 0)),
                  pl.BlockSpec((seq, aw), lambda b, i: (b, 0)),
                  pl.BlockSpec((seq, aw), lambda b, i: (b, 0)),
                  _const_spec((blk, LANES + blk))],
        out_specs=pl.BlockSpec((blk, aw), lambda b, i: (b * nq + i, 0)),
        out_shape=jax.ShapeDtypeStruct((m, aw), BF16),
        scratch_shapes=[pltpu.VMEM((nh * blk, LANES), F32), pltpu.VMEM((blk, aw), F32)],
        compiler_params=_params("parallel", "arbitrary"),
        name="attn_prompt",
    )(q, kb, vb, _cum_matrix(blk))


def _attn_sample_kernel(q_ref, kn_ref, vn_ref, ck_hbm, cv_hbm, cum_new_ref, cum_ref,
                        o_ref, carry_ref, acc_ref, kbuf, vbuf, sem, *, layer, t, blk, past, heads):
    b = pl.program_id(0)
    nblk = past // blk

    def slot_of(kb):
        return lax.rem(nblk - 1 - kb, 2)

    def copies(kb):
        slot = slot_of(kb)
        rows = pl.ds(pl.multiple_of(kb * blk, blk), blk)
        return (pltpu.make_async_copy(ck_hbm.at[layer, b, rows], kbuf.at[slot], sem.at[0, slot]),
                pltpu.make_async_copy(cv_hbm.at[layer, b, rows], vbuf.at[slot], sem.at[1, slot]))

    def start(kb):
        for c in copies(kb):
            c.start()

    def wait(kb):
        for c in copies(kb):
            c.wait()

    def rows_by_lane(buf, slot):
        return jnp.concatenate([buf[slot, :, h, :] for h in range(heads)], axis=-1).astype(BF16)

    start(nblk - 1)
    low = _head_masks(t)
    qs = _stack_heads(q_ref[...], low)
    carry_ref[...] = jnp.zeros_like(carry_ref)
    acc_ref[...] = jnp.zeros_like(acc_ref)
    _sb_block(qs, kn_ref[...], vn_ref[...], carry_ref, acc_ref,
              cum_new_ref[...], low, _causal_mask(t, t, 2))
    cum2 = cum_ref[...]

    def cond(state):
        kb, live = state
        return jnp.logical_and(kb >= 0, live > ATT_DEAD)

    def body(state):
        kb, _ = state
        wait(kb)

        @pl.when(kb > 0)
        def _():
            start(kb - 1)

        slot = slot_of(kb)
        _sb_block(qs, rows_by_lane(kbuf, slot), rows_by_lane(vbuf, slot),
                  carry_ref, acc_ref, cum2, low, None)
        return kb - 1, jnp.max(carry_ref[...])

    kb_end, _ = lax.while_loop(cond, body, (nblk - 1, jnp.max(carry_ref[...])))

    @pl.when(kb_end >= 0)
    def _():
        wait(kb_end)

    o_ref[...] = acc_ref[...].astype(o_ref.dtype)


def _attn_sample(q, kb, vb, ck, cv, *, layer, bsz, t, past):
    m, aw = q.shape
    blk = ATT_BLOCK
    nh = aw // HEAD_DIM
    new = pl.BlockSpec((t, aw), lambda b: (b, 0))
    hbm = pl.BlockSpec(memory_space=pl.ANY)
    return pl.pallas_call(
        functools.partial(_attn_sample_kernel, layer=layer, t=t, blk=blk, past=past, heads=nh),
        grid=(bsz,),
        in_specs=[new, new, new, hbm, hbm,
                  _const_spec((t, LANES + t)), _const_spec((blk, LANES + blk))],
        out_specs=new,
        out_shape=jax.ShapeDtypeStruct((m, aw), BF16),
        scratch_shapes=[pltpu.VMEM((nh * t, LANES), F32), pltpu.VMEM((t, aw), F32),
                        pltpu.VMEM((2, blk, nh, HEAD_DIM), F32), pltpu.VMEM((2, blk, nh, HEAD_DIM), F32),
                        pltpu.SemaphoreType.DMA((2, 2))],
        compiler_params=_params("arbitrary"),
        name="attn_sample",
    )(q, kb, vb, ck, cv, _cum_matrix(t), _cum_matrix(blk))


def _hgrn_group(blk, lbp, gain, tril, ind, bd_mask, st_ref):
    bw = gain.shape[1]
    n = HG_STEP
    half = n // 2
    steps = blk.shape[0] // n
    npair = bw // LANES
    pair = lambda p: slice(p * LANES, (p + 1) * LANES)
    qr = blk[:, 0:bw]
    z = blk[:, bw:2 * bw]
    v = blk[:, 2 * bw:3 * bw]
    gr = blk[:, 3 * bw:4 * bw]
    log_lb, log1m_lb, one_m_lb = lbp[0:1], lbp[1:2], lbp[2:3]

    sp = _log1p_exp_neg_abs(z)
    cc = log1m_lb + (jnp.minimum(z, 0.0) - sp)
    log_f = jnp.maximum(log_lb, cc) + jnp.log1p(jnp.exp(-jnp.abs(log_lb - cc)))
    k = one_m_lb * jnp.exp(jnp.minimum(-z, 0.0) - sp)
    q = qr * jax.nn.sigmoid(qr)
    b = _ldot_x3(tril, log_f)
    vb = v.astype(BF16)

    row = lax.broadcasted_iota(jnp.int32, (n, bw), 0)
    row_hi = lax.broadcasted_iota(jnp.int32, (half, bw), 0) + half
    parts, qbs, decays, upds = [], [], [], []
    for j in range(steps):
        sl = slice(j * n, (j + 1) * n)
        bj, qj, kj = b[sl], q[sl], k[sl]
        b_last = bj[n - 1:n]
        qbs.append((qj * jnp.exp(bj)).astype(BF16))
        kd = (kj * jnp.exp(b_last - bj)).astype(BF16)
        decays.append(jnp.exp(b_last))
        upds.append([lax.dot_general(vb[sl, pair(p)], kd[:, pair(p)], (((0,), (0,)), ((), ())),
                                     preferred_element_type=F32) for p in range(npair)])
        for s in range(half):
            e = jnp.exp(jnp.where(row >= s, bj - bj[s:s + 1], NEG_BIG))
            parts.append((qj * e) * kj[s:s + 1])
        bh, qh = bj[half:], qj[half:]
        for s in range(half, n):
            e = jnp.exp(jnp.where(row_hi >= s, bh - bj[s:s + 1], NEG_BIG))
            parts.append((qh * e) * kj[s:s + 1])
    att = _dot(jnp.concatenate(parts, axis=0).astype(BF16), ind)
    per_step = half * n + half * half

    sts = [st_ref[p] for p in range(npair)]
    outs = []
    for j in range(steps):
        base = j * per_step
        vj = v[j * n:(j + 1) * n]
        o_lo = att[base:base + n] * vj[0:1]
        for s in range(1, half):
            o_lo = o_lo + att[base + s * n:base + (s + 1) * n] * vj[s:s + 1]
        base += half * n
        o_hi = att[base:base + half] * vj[half:half + 1]
        for s in range(half + 1, n):
            o_hi = o_hi + att[base + (s - half) * half:base + (s - half + 1) * half] * vj[s:s + 1]
        o_intra = jnp.concatenate([o_lo[:half], o_lo[half:] + o_hi], axis=0)
        o_inter = []
        for p in range(npair):
            o_inter.append(_dot_nt(qbs[j][:, pair(p)], sts[p].astype(BF16)))
            sts[p] = sts[p] * decays[j][:, pair(p)] + jnp.where(bd_mask, upds[j][p], 0.0)
        outs.append(o_intra + jnp.concatenate(o_inter, axis=1))
    for p in range(npair):
        st_ref[p] = sts[p]
    o = jnp.concatenate(outs, axis=0) if steps > 1 else outs[0]

    ms = _dot_x2(o * o, ind) * (1.0 / HEAD_DIM)
    o = o * lax.rsqrt(ms + EPS) * gain
    return o * (gr * jax.nn.sigmoid(gr))


def _hgrn_kernel(b_ref, lbp_ref, gain_ref, tril_ref, ind_ref, st0_ref, o_ref, st_out_ref, st_ref,
                 *, groups, grows):
    i = pl.program_id(1)

    @pl.when(i == 0)
    def _():
        st_ref[...] = st0_ref[...]

    lbp = lbp_ref[...]
    gain = gain_ref[...]
    tril = tril_ref[...]
    ind = ind_ref[...]
    r = lax.broadcasted_iota(jnp.int32, (LANES, LANES), 0) // HEAD_DIM
    c = lax.broadcasted_iota(jnp.int32, (LANES, LANES), 1) // HEAD_DIM
    bd_mask = r == c

    def body(j, _):
        rows = pl.ds(pl.multiple_of(j * grows, grows), grows)
        o = _hgrn_group(b_ref[rows, :], lbp, gain, tril, ind, bd_mask, st_ref)
        o_ref[rows, :] = o.astype(o_ref.dtype)
        return 0

    lax.fori_loop(0, groups, body, 0)

    @pl.when(i == pl.num_programs(1) - 1)
    def _():
        st_out_ref[...] = st_ref[...]


def _hgrn(bproj, lbp, gain, st0, *, bsz, t, tc, gsteps=4):
    m, w4 = bproj.shape
    bw = w4 // 4
    tc = min(tc, t)
    nc = t // tc
    grows = min(gsteps * HG_STEP, tc)
    npair = bw // LANES
    j = jnp.arange(grows, dtype=jnp.int32)
    tril = jnp.logical_and(j[:, None] >= j[None, :],
                           j[:, None] // HG_STEP == j[None, :] // HG_STEP).astype(BF16)
    st_spec = pl.BlockSpec((None, npair, LANES, LANES), lambda b, i: (b, 0, 0, 0))
    return pl.pallas_call(
        functools.partial(_hgrn_kernel, groups=tc // grows, grows=grows),
        grid=(bsz, nc),
        in_specs=[pl.BlockSpec((tc, w4), lambda b, i: (b * nc + i, 0)),
                  _const_spec((3, bw)), _const_spec((1, bw)),
                  _const_spec((grows, grows)), _const_spec((bw, bw)), st_spec],
        out_specs=[pl.BlockSpec((tc, bw), lambda b, i: (b * nc + i, 0)), st_spec],
        out_shape=[jax.ShapeDtypeStruct((m, bw), BF16),
                   jax.ShapeDtypeStruct((bsz, npair, LANES, LANES), F32)],
        scratch_shapes=[pltpu.VMEM((npair, LANES, LANES), F32)],
        compiler_params=_params("parallel", "arbitrary"),
        name="hgrn2",
    )(bproj, lbp, gain, tril, _group_indicator(bw), st0)


def _state_to_tiles(s):
    bsz, h, dk, dv = s.shape
    st = jnp.swapaxes(s, 2, 3).reshape(bsz, h // 2, 2, dv, dk)
    z = jnp.zeros_like(st[:, :, 0])
    top = jnp.concatenate([st[:, :, 0], z], axis=-1)
    bot = jnp.concatenate([z, st[:, :, 1]], axis=-1)
    return jnp.concatenate([top, bot], axis=-2)


def _tiles_to_state(tiles):
    a = tiles[:, :, :HEAD_DIM, :HEAD_DIM]
    b = tiles[:, :, HEAD_DIM:, HEAD_DIM:]
    st = jnp.stack([a, b], axis=2)
    bsz, npair = tiles.shape[:2]
    return jnp.swapaxes(st.reshape(bsz, 2 * npair, HEAD_DIM, HEAD_DIM), 2, 3)


def _sgu_kernel(c_ref, gain_ref, w_ref, bias_ref, ind_ref, y_ref, vn_ref, *, cw, lc):
    c = c_ref[...]
    u = jax.nn.gelu(c[:, :cw])
    gv = jax.nn.gelu(c[:, cw:])
    vn = _group_rmsnorm(gv, ind_ref[...], gain_ref[...])
    vn_ref[...] = vn
    low = _head_masks(lc)
    bias = bias_ref[...]
    for j in range(c.shape[0] // lc):
        rows = slice(j * lc, (j + 1) * lc)
        mixed = []
        for p in range(cw // LANES):
            vp = vn[rows, p * LANES:(p + 1) * LANES].astype(BF16)
            mixed.append(jnp.where(low, _dot(w_ref[2 * p], vp), _dot(w_ref[2 * p + 1], vp)))
        y_ref[rows, :] = (u[rows] * (jnp.concatenate(mixed, axis=1) + bias)).astype(y_ref.dtype)


def _sgu(cproj, gain, w, bias, *, lc, tm):
    m, w2 = cproj.shape
    cw = w2 // 2
    ng = cw // HEAD_DIM
    tm = min(tm, m)
    row = lambda width: pl.BlockSpec((tm, width), lambda i: (i, 0))
    return pl.pallas_call(
        functools.partial(_sgu_kernel, cw=cw, lc=lc),
        grid=(m // tm,),
        in_specs=[row(w2), _const_spec((1, cw)), _const_spec((ng, lc, lc)),
                  _const_spec((lc, cw)), _const_spec((cw, cw))],
        out_specs=[row(cw), row(cw)],
        out_shape=[jax.ShapeDtypeStruct((m, cw), BF16), jax.ShapeDtypeStruct((m, cw), F32)],
        compiler_params=_params("parallel"),
        name="sgu",
    )(cproj, gain, w, bias, _group_indicator(cw))


def _mix_residual(x_ref, oa_ref, ob_ref, oc_ref, wo_ref):
    aw = oa_ref.shape[1]
    bw = ob_ref.shape[1]
    return (x_ref[...] + _dot(oa_ref[...], wo_ref[0:aw, :])
            + _dot(ob_ref[...], wo_ref[aw:aw + bw, :])
            + _dot(oc_ref[...], wo_ref[aw + bw:, :]))


def _rms(h, g):
    return h * lax.rsqrt(jnp.mean(h * h, axis=-1, keepdims=True) + EPS) * g


def _ffn_dense_kernel(x_ref, oa_ref, ob_ref, oc_ref, wo_ref, g_ref, wg_ref, wu_ref, wd_ref, y_ref,
                      *, fchunk):
    h = _mix_residual(x_ref, oa_ref, ob_ref, oc_ref, wo_ref)
    hn = _rms(h, g_ref[...]).astype(BF16)
    y_ref[...] = h
    for f0 in range(0, wg_ref.shape[1], fchunk):
        cols = slice(f0, f0 + fchunk)
        gate = _dot(hn, wg_ref[:, cols])
        up = _dot(hn, wu_ref[:, cols])
        act = (gate * jax.nn.sigmoid(gate) * up).astype(BF16)
        y_ref[...] += _dot(act, wd_ref[cols, :])


def _ffn_dense(x, oa, ob, oc, wo, g, wg, wu, wd, *, tm):
    m, d = x.shape
    tm = min(tm, m)
    ff = wg.shape[1]
    fchunk = ff // 2 if (ff // 2) % LANES == 0 else ff
    row = lambda a: pl.BlockSpec((tm, a.shape[1]), lambda i: (i, 0))
    return pl.pallas_call(
        functools.partial(_ffn_dense_kernel, fchunk=fchunk),
        grid=(m // tm,),
        in_specs=[row(x), row(oa), row(ob), row(oc), _const_spec(wo.shape), _const_spec((1, d)),
                  _const_spec(wg.shape), _const_spec(wu.shape), _const_spec(wd.shape)],
        out_specs=row(x),
        out_shape=jax.ShapeDtypeStruct((m, d), F32),
        compiler_params=_params("parallel"),
        name="ffn_dense",
    )(x, oa, ob, oc, wo, g.reshape(1, d), wg, wu, wd)


def _route(hn, wr):
    logits = jnp.dot(hn, wr, preferred_element_type=F32, precision=lax.Precision.HIGHEST)
    lane = lax.broadcasted_iota(jnp.int32, logits.shape, 1).astype(F32)
    logits = jnp.where(lane < N_EXPERTS, logits, -jnp.inf)
    m1 = jnp.max(logits, axis=-1, keepdims=True)
    i1 = jnp.min(jnp.where(logits == m1, lane, float(LANES)), axis=-1, keepdims=True)
    rest = jnp.where(lane == i1, -jnp.inf, logits)
    m2 = jnp.max(rest, axis=-1, keepdims=True)
    i2 = jnp.min(jnp.where(rest == m2, lane, float(LANES)), axis=-1, keepdims=True)
    e2 = jnp.exp(m2 - m1)
    g1 = 1.0 / (1.0 + e2)
    g2 = e2 / (1.0 + e2)
    return jnp.where(lane == i1, g1, 0.0) + jnp.where(lane == i2, g2, 0.0)


def _ffn_moe_kernel(x_ref, oa_ref, ob_ref, oc_ref, wo_ref, g_ref, wr_ref, wg_ref, wu_ref, wd_ref,
                    y_ref, h_ref, hn_ref, comb_ref, acc_ref):
    e = pl.program_id(1)

    @pl.when(e == 0)
    def _():
        h = _mix_residual(x_ref, oa_ref, ob_ref, oc_ref, wo_ref)
        hn = _rms(h, g_ref[...])
        h_ref[...] = h
        hn_ref[...] = hn.astype(BF16)
        comb = _route(hn, wr_ref[...])
        for j in range(N_EXPERTS):
            comb_ref[j] = jnp.broadcast_to(comb[:, j:j + 1], comb.shape)
        acc_ref[...] = jnp.zeros_like(acc_ref)

    hn = hn_ref[...]
    gate = _dot(hn, wg_ref[...])
    up = _dot(hn, wu_ref[...])
    act = (gate * jax.nn.sigmoid(gate) * up).astype(BF16)
    out = _dot(act, wd_ref[...])
    cb = comb_ref[e]
    for j in range(out.shape[1] // LANES):
        lanes = slice(j * LANES, (j + 1) * LANES)
        acc_ref[:, lanes] += cb * out[:, lanes]

    @pl.when(e == pl.num_programs(1) - 1)
    def _():
        y_ref[...] = h_ref[...] + acc_ref[...]


def _ffn_moe(x, oa, ob, oc, wo, g, wr, wg, wu, wd, *, tm):
    m, d = x.shape
    tm = min(tm, m)
    ne, _, ff = wg.shape
    row = lambda a: pl.BlockSpec((tm, a.shape[1]), lambda i, e: (i, 0))
    const = lambda shape: pl.BlockSpec(shape, lambda i, e: (0,) * len(shape))
    return pl.pallas_call(
        _ffn_moe_kernel,
        grid=(m // tm, ne),
        in_specs=[row(x), row(oa), row(ob), row(oc), const(wo.shape), const((1, d)),
                  const(wr.shape),
                  pl.BlockSpec((None, d, ff), lambda i, e: (e, 0, 0)),
                  pl.BlockSpec((None, d, ff), lambda i, e: (e, 0, 0)),
                  pl.BlockSpec((None, ff, d), lambda i, e: (e, 0, 0))],
        out_specs=row(x),
        out_shape=jax.ShapeDtypeStruct((m, d), F32),
        scratch_shapes=[pltpu.VMEM((tm, d), F32), pltpu.VMEM((tm, d), BF16),
                        pltpu.VMEM((ne, tm, LANES), F32), pltpu.VMEM((tm, d), F32)],
        compiler_params=_params("parallel", "arbitrary"),
        name="ffn_moe",
    )(x, oa, ob, oc, wo, g.reshape(1, d), wr, wg, wu, wd)


def _tile_gain(g, width):
    return jnp.tile(g.astype(F32), width // g.shape[0]).reshape(1, width)


def kernel(x_prompt, x_sample, cache_k, cache_v, state_b, w_in, w_out, norm_mix, norm_ffn,
           q_norm, k_norm, hgrn_lb, hgrn_out_norm, sgu_norm, sgu_w, sgu_b,
           ffn_w_gate, ffn_w_up, ffn_w_down, moe_router, moe_w_gate, moe_w_up, moe_w_down):
    depth = w_in.shape[0]
    bsz, seq, d = x_prompt.shape
    dbsz, dseq, _ = x_sample.shape
    past = cache_k.shape[2]
    a_heads = cache_k.shape[3]
    b_heads = state_b.shape[2]
    aw = a_heads * HEAD_DIM
    bw = b_heads * HEAD_DIM
    c_groups = sgu_w.shape[1]
    cw = c_groups * HEAD_DIM
    c_chunk = sgu_w.shape[2]

    cs = jnp.cumsum(jax.nn.softmax(hgrn_lb.astype(F32), axis=0), axis=0)
    lb_all = cs - cs[0:1]

    def layer(l, xp, xs):
        w_in_l = w_in[l].astype(BF16)
        w_out_l = w_out[l].astype(BF16)
        qg = _tile_gain(q_norm[l], aw)
        kg = _tile_gain(k_norm[l], aw)
        hg = _tile_gain(hgrn_out_norm[l], bw)
        sg = _tile_gain(sgu_norm[l], cw)
        lb = lb_all[l]
        lbp = jnp.stack([jnp.log(lb), jnp.log1p(-lb), 1.0 - lb])

        def sgu_params(lc):
            tril = jnp.tril(jnp.ones((lc, lc), sgu_w.dtype))
            w = (sgu_w[l][:, :lc, :lc] * tril).astype(BF16)
            bias = jnp.repeat(sgu_b[l][:, :lc].T, HEAD_DIM, axis=1)
            return w, bias

        def ffn(x, oa, ob, oc):
            i = l // 2
            if l % 2 == 0:
                return _ffn_dense(x, oa, ob, oc, w_out_l, norm_ffn[l], ffn_w_gate[i].astype(BF16),
                                  ffn_w_up[i].astype(BF16), ffn_w_down[i].astype(BF16), tm=512)
            wr = jnp.zeros((d, LANES), F32).at[:, :N_EXPERTS].set(moe_router[i])
            return _ffn_moe(x, oa, ob, oc, w_out_l, norm_ffn[l], wr, moe_w_gate[i].astype(BF16),
                            moe_w_up[i].astype(BF16), moe_w_down[i].astype(BF16), tm=512)

        x = xp.reshape(bsz * seq, d)
        q, k, v, kb, vb, bp, cp = _inproj(x, norm_mix[l], w_in_l, qg, kg, aw=aw, bw=bw, cw=cw, tm=512)
        oa = _attn_prompt(q, kb, vb, bsz=bsz, seq=seq)
        st0 = jnp.zeros((bsz, bw // LANES, LANES, LANES), F32)
        ob, st = _hgrn(bp, lbp, hg, st0, bsz=bsz, t=seq, tc=512)
        sw, sb = sgu_params(c_chunk)
        oc, _ = _sgu(cp, sg, sw, sb, lc=c_chunk, tm=8 * c_chunk)
        yp = ffn(x, oa, ob, oc).reshape(bsz, seq, d)
        outs_p = (k.reshape(bsz, seq, a_heads, HEAD_DIM), v.reshape(bsz, seq, a_heads, HEAD_DIM),
                  _tiles_to_state(st))

        x = xs.reshape(dbsz * dseq, d)
        q, k, v, kb, vb, bp, cp = _inproj(x, norm_mix[l], w_in_l, qg, kg, aw=aw, bw=bw, cw=cw, tm=512)
        oa = _attn_sample(q, kb, vb, cache_k, cache_v, layer=l, bsz=dbsz, t=dseq, past=past)
        ob, st = _hgrn(bp, lbp, hg, _state_to_tiles(state_b[l].astype(F32)), bsz=dbsz, t=dseq, tc=dseq)
        sw, sb = sgu_params(dseq)
        oc, vn = _sgu(cp, sg, sw, sb, lc=dseq, tm=dbsz * dseq)
        ys = ffn(x, oa, ob, oc).reshape(dbsz, dseq, d)
        outs_s = (k.reshape(dbsz, dseq, a_heads, HEAD_DIM), v.reshape(dbsz, dseq, a_heads, HEAD_DIM),
                  _tiles_to_state(st), vn.reshape(dbsz, dseq, cw))
        return yp, ys, outs_p, outs_s

    yp, ys = x_prompt, x_sample
    ps, ss = [], []
    for l in range(depth):
        yp, ys, op, os_ = layer(l, yp, ys)
        ps.append(op)
        ss.append(os_)
    stack = lambda items, j: jnp.stack([it[j] for it in items])
    return (yp, ys, stack(ps, 0), stack(ps, 1), stack(ps, 2),
            stack(ss, 0), stack(ss, 1), stack(ss, 2), stack(ss, 3))
```

```python
import functools

import jax
import jax.numpy as jnp
from jax import lax
from jax.experimental import pallas as pl
from jax.experimental.pallas import tpu as pltpu

F32 = jnp.float32
BF16 = jnp.bfloat16

HEAD_DIM = 64
LANES = 128
EPS = 1e-6
N_EXPERTS = 8
TOP_K = 2
VMEM_LIMIT = 56 * 1024 * 1024

ATT_BLOCK = 128
ATT_DEAD = -120.0
HG_STEP = 16
NEG_BIG = -1e30


def _dot(a, b):
    return jnp.dot(a, b, preferred_element_type=F32)


def _dot_nt(a, b):
    return lax.dot_general(a, b, (((1,), (1,)), ((), ())), preferred_element_type=F32)


def _split2(x):
    hi = x.astype(BF16)
    lo = (x - hi.astype(F32)).astype(BF16)
    return hi, lo


def _split3(x):
    hi = x.astype(BF16)
    r = x - hi.astype(F32)
    mid = r.astype(BF16)
    lo = (r - mid.astype(F32)).astype(BF16)
    return hi, mid, lo


def _dot_x2(x, m):
    hi, lo = _split2(x)
    return _dot(hi, m) + _dot(lo, m)


def _ldot_x3(m, x):
    hi, mid, lo = _split3(x)
    return _dot(m, hi) + _dot(m, mid) + _dot(m, lo)


def _log1p_exp_neg_abs(z):
    return jnp.log1p(jnp.exp(-jnp.abs(z)))


def _group_rmsnorm(a, ind, gain_row):
    ms = _dot_x2(a * a, ind) * (1.0 / HEAD_DIM)
    return a * lax.rsqrt(ms + EPS) * gain_row


def _group_indicator(width):
    g = jnp.arange(width, dtype=jnp.int32) // HEAD_DIM
    return (g[:, None] == g[None, :]).astype(BF16)


def _const_spec(shape):
    nd = len(shape)
    return pl.BlockSpec(shape, lambda *_: (0,) * nd)


def _params(*sem):
    return pltpu.CompilerParams(dimension_semantics=sem, vmem_limit_bytes=VMEM_LIMIT)


def _inproj_kernel(x_ref, g_ref, w_ref, qg_ref, kg_ref, ind_ref, *rest, aw, bw, key_major):
    q_ref, k_ref, v_ref, kb_ref, vb_ref, b_ref, c_ref = rest[-7:]
    x = x_ref[...]
    ms = jnp.mean(x * x, axis=-1, keepdims=True)
    xn = (x * lax.rsqrt(ms + EPS) * g_ref[...]).astype(BF16)
    ind = ind_ref[...]
    aq = _dot(xn, w_ref[:, 0:aw])
    q = _group_rmsnorm(aq, ind, qg_ref[...])
    q_ref[...] = (q * (HEAD_DIM ** -0.5)).astype(BF16)
    ak = _dot(xn, w_ref[:, aw:2 * aw])
    k = _group_rmsnorm(ak, ind, kg_ref[...])
    v = _dot(xn, w_ref[:, 2 * aw:3 * aw])
    if key_major:
        k = k.T
        v = v.T
    k_ref[...] = k
    v_ref[...] = v
    kb_ref[...] = k.astype(BF16)
    vb_ref[...] = v.astype(BF16)
    b_ref[...] = _dot(xn, w_ref[:, 3 * aw:3 * aw + 4 * bw])
    c_ref[...] = _dot(xn, w_ref[:, 3 * aw + 4 * bw:])


def _inproj(x, g, w, qg, kg, *, aw, bw, cw, tm, key_major=None):
    m, d = x.shape
    tm = min(tm, m)
    nw = w.shape[1]
    row = lambda width: pl.BlockSpec((tm, width), lambda i: (i, 0))
    in_specs = [row(d), _const_spec((1, d)), _const_spec((d, nw)),
                _const_spec((1, aw)), _const_spec((1, aw)), _const_spec((aw, aw))]
    args = [x, g.reshape(1, d), w, qg, kg, _group_indicator(aw)]
    aliases = {}
    if key_major is None:
        kv_specs = [row(aw)] * 4
        kv_shapes = [jax.ShapeDtypeStruct((m, aw), F32)] * 2 + [jax.ShapeDtypeStruct((m, aw), BF16)] * 2
    else:
        layer, depth, bsz, seq, k_all, v_all = key_major
        nt = seq // tm
        kv_specs = ([pl.BlockSpec((None, None, aw, tm), lambda i: (layer, i // nt, 0, i % nt))] * 2
                    + [pl.BlockSpec((None, aw, tm), lambda i: (i // nt, 0, i % nt))] * 2)
        kv_shapes = ([jax.ShapeDtypeStruct((depth, bsz, aw, seq), F32)] * 2
                     + [jax.ShapeDtypeStruct((bsz, aw, seq), BF16)] * 2)
        in_specs += [pl.BlockSpec(memory_space=pl.ANY)] * 2
        aliases = {len(args): 1, len(args) + 1: 2}
        args += [k_all, v_all]
    return pl.pallas_call(
        functools.partial(_inproj_kernel, aw=aw, bw=bw, key_major=key_major is not None),
        grid=(m // tm,),
        in_specs=in_specs,
        out_specs=[row(aw)] + kv_specs + [row(4 * bw), row(2 * cw)],
        out_shape=([jax.ShapeDtypeStruct((m, aw), BF16)] + kv_shapes
                   + [jax.ShapeDtypeStruct((m, 4 * bw), F32), jax.ShapeDtypeStruct((m, 2 * cw), F32)]),
        input_output_aliases=aliases,
        compiler_params=_params("parallel"),
        name="inproj",
    )(*args)


def _head_masks(rows):
    lane = lax.broadcasted_iota(jnp.int32, (rows, LANES), 1)
    return lane < HEAD_DIM


def _sb_block(qs, kblk, vblk, carry_ref, acc_ref, cum2, low, mask):
    tk = kblk.shape[1]
    rows = qs[0].shape[0] // 2
    n = 2 * rows
    npair = len(qs)
    pair = lambda p: slice(p * LANES, (p + 1) * LANES)
    zs = [_dot(q, kblk[pair(p), :]) for p, q in enumerate(qs)]
    log_betas, rs = [], []
    for p in range(npair):
        z = zs[p]
        sp = jnp.log(1.0 + jnp.exp(-jnp.abs(z)))
        log_beta = jnp.minimum(z, 0.0) - sp
        log_keep = log_beta - z
        if mask is not None:
            log_keep = jnp.where(mask, log_keep, 0.0)
        hi, lo = _split2(log_keep)
        r = _dot(jnp.concatenate([hi, lo], axis=0), cum2)
        log_betas.append(log_beta)
        rs.append(r[:n] + r[n:])
    for p in range(npair):
        carry = carry_ref[p * n:(p + 1) * n, :]
        after = rs[p][:, LANES:] + (carry if tk == LANES else carry[:, :tk])
        w = jnp.exp(log_betas[p] + after)
        if mask is not None:
            w = jnp.where(mask, w, 0.0)
        pv = _dot_nt(w.astype(BF16), vblk[pair(p), :])
        acc_ref[:, pair(p)] += jnp.where(low, pv[:rows], pv[rows:])
        carry_ref[p * n:(p + 1) * n, :] = carry + rs[p][:, :LANES]


def _stack_heads(q, low):
    qs = []
    for p in range(q.shape[1] // LANES):
        qp = q[:, p * LANES:(p + 1) * LANES]
        zero = jnp.zeros_like(qp)
        qs.append(jnp.concatenate([jnp.where(low, qp, zero), jnp.where(low, zero, qp)], axis=0))
    return qs


def _causal_mask(rows, tk, heads):
    r = lax.broadcasted_iota(jnp.int32, (heads * rows, tk), 0) & (rows - 1)
    c = lax.broadcasted_iota(jnp.int32, (heads * rows, tk), 1)
    return c < r


def _attn_prompt_kernel(q_ref, k_ref, v_ref, cum_ref, o_ref, carry_ref, acc_ref, *, blk, heads):
    i = pl.program_id(1)
    low = _head_masks(blk)
    qs = _stack_heads(q_ref[...], low)
    carry_ref[...] = jnp.zeros_like(carry_ref)
    acc_ref[...] = jnp.zeros_like(acc_ref)
    cum2 = cum_ref[...]
    start = pl.multiple_of(i * blk, blk)
    _sb_block(qs, k_ref[:, pl.ds(start, blk)], v_ref[:, pl.ds(start, blk)],
              carry_ref, acc_ref, cum2, low, _causal_mask(blk, blk, 2))

    def cond(state):
        kb, live = state
        return jnp.logical_and(kb >= 0, live > ATT_DEAD)

    def body(state):
        kb, _ = state
        s = pl.multiple_of(kb * blk, blk)
        _sb_block(qs, k_ref[:, pl.ds(s, blk)], v_ref[:, pl.ds(s, blk)],
                  carry_ref, acc_ref, cum2, low, None)
        return kb - 1, jnp.max(carry_ref[...])

    lax.while_loop(cond, body, (i - 1, jnp.max(carry_ref[...])))
    o_ref[...] = acc_ref[...].astype(o_ref.dtype)


def _cum_matrix(n):
    j = jnp.arange(n, dtype=jnp.int32)
    newer = (j[:, None] > j[None, :]).astype(BF16)
    return jnp.concatenate([jnp.ones((n, LANES), BF16), newer], axis=1)


def _attn_prompt(q, kb, vb, *, bsz, seq):
    m, aw = q.shape
    blk = ATT_BLOCK
    nq = seq // blk
    nh = aw // HEAD_DIM
    return pl.pallas_call(
        functools.partial(_attn_prompt_kernel, blk=blk, heads=nh),
        grid=(bsz, nq),
        in_specs=[pl.BlockSpec((blk, aw), lambda b, i: (b * nq + i, 0)),
                  pl.BlockSpec((None, aw, seq), lambda b, i: (b, 0, 0)),
                  pl.BlockSpec((None, aw, seq), lambda b, i: (b, 0, 0)),
                  _const_spec((blk, LANES + blk))],
        out_specs=pl.BlockSpec((blk, aw), lambda b, i: (b * nq + i, 0)),
        out_shape=jax.ShapeDtypeStruct((m, aw), BF16),
        scratch_shapes=[pltpu.VMEM((nh * blk, LANES), F32), pltpu.VMEM((blk, aw), F32)],
        compiler_params=_params("parallel", "arbitrary"),
        name="attn_prompt",
    )(q, kb, vb, _cum_matrix(blk))


def _attn_sample_kernel(q_ref, kn_ref, vn_ref, ck_hbm, cv_hbm, cum_new_ref, cum_ref,
                        o_ref, carry_ref, acc_ref, kbuf, vbuf, sem, *, layer, t, blk, past, heads):
    b = pl.program_id(0)
    nblk = past // blk

    def slot_of(kb):
        return lax.rem(nblk - 1 - kb, 2)

    def copies(kb):
        slot = slot_of(kb)
        cols = pl.ds(pl.multiple_of(kb * blk, blk), blk)
        return (pltpu.make_async_copy(ck_hbm.at[layer, b, :, cols], kbuf.at[slot], sem.at[0, slot]),
                pltpu.make_async_copy(cv_hbm.at[layer, b, :, cols], vbuf.at[slot], sem.at[1, slot]))

    def start(kb):
        for c in copies(kb):
            c.start()

    def wait(kb):
        for c in copies(kb):
            c.wait()

    start(nblk - 1)
    low = _head_masks(t)
    qs = _stack_heads(q_ref[...], low)
    carry_ref[...] = jnp.zeros_like(carry_ref)
    acc_ref[...] = jnp.zeros_like(acc_ref)
    _sb_block(qs, kn_ref[...], vn_ref[...], carry_ref, acc_ref,
              cum_new_ref[...], low, _causal_mask(t, t, 2))
    cum2 = cum_ref[...]

    def cond(state):
        kb, live = state
        return jnp.logical_and(kb >= 0, live > ATT_DEAD)

    def body(state):
        kb, _ = state
        wait(kb)

        @pl.when(kb > 0)
        def _():
            start(kb - 1)

        slot = slot_of(kb)
        _sb_block(qs, kbuf[slot].astype(BF16), vbuf[slot].astype(BF16),
                  carry_ref, acc_ref, cum2, low, None)
        return kb - 1, jnp.max(carry_ref[...])

    kb_end, _ = lax.while_loop(cond, body, (nblk - 1, jnp.max(carry_ref[...])))

    @pl.when(kb_end >= 0)
    def _():
        wait(kb_end)

    o_ref[...] = acc_ref[...].astype(o_ref.dtype)


def _attn_sample(q, kb, vb, ck, cv, *, layer, bsz, t, past):
    m, aw = q.shape
    blk = ATT_BLOCK
    nh = aw // HEAD_DIM
    assert past >= blk and past % blk == 0
    new = pl.BlockSpec((t, aw), lambda b: (b, 0))
    new_t = pl.BlockSpec((None, aw, t), lambda b: (b, 0, 0))
    hbm = pl.BlockSpec(memory_space=pl.ANY)
    return pl.pallas_call(
        functools.partial(_attn_sample_kernel, layer=layer, t=t, blk=blk, past=past, heads=nh),
        grid=(bsz,),
        in_specs=[new, new_t, new_t, hbm, hbm,
                  _const_spec((t, LANES + t)), _const_spec((blk, LANES + blk))],
        out_specs=new,
        out_shape=jax.ShapeDtypeStruct((m, aw), BF16),
        scratch_shapes=[pltpu.VMEM((nh * t, LANES), F32), pltpu.VMEM((t, aw), F32),
                        pltpu.VMEM((2, aw, blk), F32), pltpu.VMEM((2, aw, blk), F32),
                        pltpu.SemaphoreType.DMA((2, 2))],
        compiler_params=_params("arbitrary"),
        name="attn_sample",
    )(q, kb, vb, ck, cv, _cum_matrix(t), _cum_matrix(blk))


def _hgrn_group(blk, lbp, gain, tril, ind, bd_mask, st_ref):
    bw = gain.shape[1]
    n = HG_STEP
    half = n // 2
    steps = blk.shape[0] // n
    npair = bw // LANES
    pair = lambda p: slice(p * LANES, (p + 1) * LANES)
    qr = blk[:, 0:bw]
    z = blk[:, bw:2 * bw]
    v = blk[:, 2 * bw:3 * bw]
    gr = blk[:, 3 * bw:4 * bw]
    log_lb, log1m_lb, one_m_lb = lbp[0:1], lbp[1:2], lbp[2:3]

    sp = _log1p_exp_neg_abs(z)
    cc = log1m_lb + (jnp.minimum(z, 0.0) - sp)
    log_f = jnp.maximum(log_lb, cc) + jnp.log1p(jnp.exp(-jnp.abs(log_lb - cc)))
    k = one_m_lb * jnp.exp(jnp.minimum(-z, 0.0) - sp)
    q = qr * jax.nn.sigmoid(qr)
    b = _ldot_x3(tril, log_f)
    vb = v.astype(BF16)

    row = lax.broadcasted_iota(jnp.int32, (n, bw), 0)
    row_hi = lax.broadcasted_iota(jnp.int32, (half, bw), 0) + half
    parts, qbs, decays, upds = [], [], [], []
    for j in range(steps):
        sl = slice(j * n, (j + 1) * n)
        bj, qj, kj = b[sl], q[sl], k[sl]
        b_last = bj[n - 1:n]
        qbs.append((qj * jnp.exp(bj)).astype(BF16))
        kd = (kj * jnp.exp(b_last - bj)).astype(BF16)
        decays.append(jnp.exp(b_last))
        upds.append([lax.dot_general(vb[sl, pair(p)], kd[:, pair(p)], (((0,), (0,)), ((), ())),
                                     preferred_element_type=F32) for p in range(npair)])
        for s in range(half):
            e = jnp.exp(jnp.where(row >= s, bj - bj[s:s + 1], NEG_BIG))
            parts.append((qj * e) * kj[s:s + 1])
        bh, qh = bj[half:], qj[half:]
        for s in range(half, n):
            e = jnp.exp(jnp.where(row_hi >= s, bh - bj[s:s + 1], NEG_BIG))
            parts.append((qh * e) * kj[s:s + 1])
    att = _dot(jnp.concatenate(parts, axis=0).astype(BF16), ind)
    per_step = half * n + half * half

    sts = [st_ref[p] for p in range(npair)]
    outs = []
    for j in range(steps):
        base = j * per_step
        vj = v[j * n:(j + 1) * n]
        o_lo = att[base:base + n] * vj[0:1]
        for s in range(1, half):
            o_lo = o_lo + att[base + s * n:base + (s + 1) * n] * vj[s:s + 1]
        base += half * n
        o_hi = att[base:base + half] * vj[half:half + 1]
        for s in range(half + 1, n):
            o_hi = o_hi + att[base + (s - half) * half:base + (s - half + 1) * half] * vj[s:s + 1]
        o_intra = jnp.concatenate([o_lo[:half], o_lo[half:] + o_hi], axis=0)
        o_inter = []
        for p in range(npair):
            o_inter.append(_dot_nt(qbs[j][:, pair(p)], sts[p].astype(BF16)))
            sts[p] = sts[p] * decays[j][:, pair(p)] + jnp.where(bd_mask, upds[j][p], 0.0)
        outs.append(o_intra + jnp.concatenate(o_inter, axis=1))
    for p in range(npair):
        st_ref[p] = sts[p]
    o = jnp.concatenate(outs, axis=0) if steps > 1 else outs[0]

    ms = _dot_x2(o * o, ind) * (1.0 / HEAD_DIM)
    o = o * lax.rsqrt(ms + EPS) * gain
    return o * (gr * jax.nn.sigmoid(gr))


def _hgrn_kernel(b_ref, lbp_ref, gain_ref, tril_ref, ind_ref, st0_ref, o_ref, st_out_ref, st_ref,
                 *, groups, grows):
    i = pl.program_id(1)

    @pl.when(i == 0)
    def _():
        st_ref[...] = st0_ref[...]

    lbp = lbp_ref[...]
    gain = gain_ref[...]
    tril = tril_ref[...]
    ind = ind_ref[...]
    r = lax.broadcasted_iota(jnp.int32, (LANES, LANES), 0) // HEAD_DIM
    c = lax.broadcasted_iota(jnp.int32, (LANES, LANES), 1) // HEAD_DIM
    bd_mask = r == c

    def body(j, _):
        rows = pl.ds(pl.multiple_of(j * grows, grows), grows)
        o = _hgrn_group(b_ref[rows, :], lbp, gain, tril, ind, bd_mask, st_ref)
        o_ref[rows, :] = o.astype(o_ref.dtype)
        return 0

    lax.fori_loop(0, groups, body, 0)

    @pl.when(i == pl.num_programs(1) - 1)
    def _():
        st_out_ref[...] = st_ref[...]


def _hgrn(bproj, lbp, gain, st0, *, bsz, t, tc, gsteps=4):
    m, w4 = bproj.shape
    bw = w4 // 4
    tc = min(tc, t)
    nc = t // tc
    grows = min(gsteps * HG_STEP, tc)
    npair = bw // LANES
    j = jnp.arange(grows, dtype=jnp.int32)
    tril = jnp.logical_and(j[:, None] >= j[None, :],
                           j[:, None] // HG_STEP == j[None, :] // HG_STEP).astype(BF16)
    st_spec = pl.BlockSpec((None, npair, LANES, LANES), lambda b, i: (b, 0, 0, 0))
    return pl.pallas_call(
        functools.partial(_hgrn_kernel, groups=tc // grows, grows=grows),
        grid=(bsz, nc),
        in_specs=[pl.BlockSpec((tc, w4), lambda b, i: (b * nc + i, 0)),
                  _const_spec((3, bw)), _const_spec((1, bw)),
                  _const_spec((grows, grows)), _const_spec((bw, bw)), st_spec],
        out_specs=[pl.BlockSpec((tc, bw), lambda b, i: (b * nc + i, 0)), st_spec],
        out_shape=[jax.ShapeDtypeStruct((m, bw), BF16),
                   jax.ShapeDtypeStruct((bsz, npair, LANES, LANES), F32)],
        scratch_shapes=[pltpu.VMEM((npair, LANES, LANES), F32)],
        compiler_params=_params("parallel", "arbitrary"),
        name="hgrn2",
    )(bproj, lbp, gain, tril, _group_indicator(bw), st0)


def _state_to_tiles(s):
    bsz, h, dk, dv = s.shape
    st = jnp.swapaxes(s, 2, 3).reshape(bsz, h // 2, 2, dv, dk)
    z = jnp.zeros_like(st[:, :, 0])
    top = jnp.concatenate([st[:, :, 0], z], axis=-1)
    bot = jnp.concatenate([z, st[:, :, 1]], axis=-1)
    return jnp.concatenate([top, bot], axis=-2)


def _tiles_to_state(tiles):
    a = tiles[:, :, :HEAD_DIM, :HEAD_DIM]
    b = tiles[:, :, HEAD_DIM:, HEAD_DIM:]
    st = jnp.stack([a, b], axis=2)
    bsz, npair = tiles.shape[:2]
    return jnp.swapaxes(st.reshape(bsz, 2 * npair, HEAD_DIM, HEAD_DIM), 2, 3)


def _sgu_kernel(c_ref, gain_ref, w_ref, bias_ref, ind_ref, y_ref, vn_ref, *, cw, lc):
    c = c_ref[...]
    u = jax.nn.gelu(c[:, :cw])
    gv = jax.nn.gelu(c[:, cw:])
    vn = _group_rmsnorm(gv, ind_ref[...], gain_ref[...])
    vn_ref[...] = vn
    low = _head_masks(lc)
    bias = bias_ref[...]
    for j in range(c.shape[0] // lc):
        rows = slice(j * lc, (j + 1) * lc)
        mixed = []
        for p in range(cw // LANES):
            vp = vn[rows, p * LANES:(p + 1) * LANES].astype(BF16)
            mixed.append(jnp.where(low, _dot(w_ref[2 * p], vp), _dot(w_ref[2 * p + 1], vp)))
        y_ref[rows, :] = (u[rows] * (jnp.concatenate(mixed, axis=1) + bias)).astype(y_ref.dtype)


def _sgu(cproj, gain, w, bias, *, lc, tm):
    m, w2 = cproj.shape
    cw = w2 // 2
    ng = cw // HEAD_DIM
    tm = min(tm, m)
    row = lambda width: pl.BlockSpec((tm, width), lambda i: (i, 0))
    return pl.pallas_call(
        functools.partial(_sgu_kernel, cw=cw, lc=lc),
        grid=(m // tm,),
        in_specs=[row(w2), _const_spec((1, cw)), _const_spec((ng, lc, lc)),
                  _const_spec((lc, cw)), _const_spec((cw, cw))],
        out_specs=[row(cw), row(cw)],
        out_shape=[jax.ShapeDtypeStruct((m, cw), BF16), jax.ShapeDtypeStruct((m, cw), F32)],
        compiler_params=_params("parallel"),
        name="sgu",
    )(cproj, gain, w, bias, _group_indicator(cw))


def _mix_residual(x_ref, oa_ref, ob_ref, oc_ref, wo_ref):
    aw = oa_ref.shape[1]
    bw = ob_ref.shape[1]
    return (x_ref[...] + _dot(oa_ref[...], wo_ref[0:aw, :])
            + _dot(ob_ref[...], wo_ref[aw:aw + bw, :])
            + _dot(oc_ref[...], wo_ref[aw + bw:, :]))


def _rms(h, g):
    return h * lax.rsqrt(jnp.mean(h * h, axis=-1, keepdims=True) + EPS) * g


def _ffn_dense_kernel(x_ref, oa_ref, ob_ref, oc_ref, wo_ref, g_ref, wg_ref, wu_ref, wd_ref, y_ref,
                      *, fchunk):
    h = _mix_residual(x_ref, oa_ref, ob_ref, oc_ref, wo_ref)
    hn = _rms(h, g_ref[...]).astype(BF16)
    y_ref[...] = h
    for f0 in range(0, wg_ref.shape[1], fchunk):
        cols = slice(f0, f0 + fchunk)
        gate = _dot(hn, wg_ref[:, cols])
        up = _dot(hn, wu_ref[:, cols])
        act = (gate * jax.nn.sigmoid(gate) * up).astype(BF16)
        y_ref[...] += _dot(act, wd_ref[cols, :])


def _ffn_dense(x, oa, ob, oc, wo, g, wg, wu, wd, *, tm):
    m, d = x.shape
    tm = min(tm, m)
    ff = wg.shape[1]
    fchunk = ff // 2 if (ff // 2) % LANES == 0 else ff
    row = lambda a: pl.BlockSpec((tm, a.shape[1]), lambda i: (i, 0))
    return pl.pallas_call(
        functools.partial(_ffn_dense_kernel, fchunk=fchunk),
        grid=(m // tm,),
        in_specs=[row(x), row(oa), row(ob), row(oc), _const_spec(wo.shape), _const_spec((1, d)),
                  _const_spec(wg.shape), _const_spec(wu.shape), _const_spec(wd.shape)],
        out_specs=row(x),
        out_shape=jax.ShapeDtypeStruct((m, d), F32),
        compiler_params=_params("parallel"),
        name="ffn_dense",
    )(x, oa, ob, oc, wo, g.reshape(1, d), wg, wu, wd)


def _route(hn, wr):
    logits = jnp.dot(hn, wr, preferred_element_type=F32, precision=lax.Precision.HIGHEST)
    lane = lax.broadcasted_iota(jnp.int32, logits.shape, 1).astype(F32)
    logits = jnp.where(lane < N_EXPERTS, logits, -jnp.inf)
    m1 = jnp.max(logits, axis=-1, keepdims=True)
    i1 = jnp.min(jnp.where(logits == m1, lane, float(LANES)), axis=-1, keepdims=True)
    rest = jnp.where(lane == i1, -jnp.inf, logits)
    m2 = jnp.max(rest, axis=-1, keepdims=True)
    i2 = jnp.min(jnp.where(rest == m2, lane, float(LANES)), axis=-1, keepdims=True)
    e2 = jnp.exp(m2 - m1)
    g1 = 1.0 / (1.0 + e2)
    g2 = e2 / (1.0 + e2)
    return jnp.where(lane == i1, g1, 0.0) + jnp.where(lane == i2, g2, 0.0)


def _ffn_moe_kernel(x_ref, oa_ref, ob_ref, oc_ref, wo_ref, g_ref, wr_ref, wg_ref, wu_ref, wd_ref,
                    y_ref, h_ref, hn_ref, comb_ref, acc_ref):
    e = pl.program_id(1)

    @pl.when(e == 0)
    def _():
        h = _mix_residual(x_ref, oa_ref, ob_ref, oc_ref, wo_ref)
        hn = _rms(h, g_ref[...])
        h_ref[...] = h
        hn_ref[...] = hn.astype(BF16)
        comb = _route(hn, wr_ref[...])
        for j in range(N_EXPERTS):
            comb_ref[j] = jnp.broadcast_to(comb[:, j:j + 1], comb.shape)
        acc_ref[...] = jnp.zeros_like(acc_ref)

    hn = hn_ref[...]
    gate = _dot(hn, wg_ref[...])
    up = _dot(hn, wu_ref[...])
    act = (gate * jax.nn.sigmoid(gate) * up).astype(BF16)
    out = _dot(act, wd_ref[...])
    cb = comb_ref[e]
    for j in range(out.shape[1] // LANES):
        lanes = slice(j * LANES, (j + 1) * LANES)
        acc_ref[:, lanes] += cb * out[:, lanes]

    @pl.when(e == pl.num_programs(1) - 1)
    def _():
        y_ref[...] = h_ref[...] + acc_ref[...]


def _ffn_moe(x, oa, ob, oc, wo, g, wr, wg, wu, wd, *, tm):
    m, d = x.shape
    tm = min(tm, m)
    ne, _, ff = wg.shape
    row = lambda a: pl.BlockSpec((tm, a.shape[1]), lambda i, e: (i, 0))
    const = lambda shape: pl.BlockSpec(shape, lambda i, e: (0,) * len(shape))
    return pl.pallas_call(
        _ffn_moe_kernel,
        grid=(m // tm, ne),
        in_specs=[row(x), row(oa), row(ob), row(oc), const(wo.shape), const((1, d)),
                  const(wr.shape),
                  pl.BlockSpec((None, d, ff), lambda i, e: (e, 0, 0)),
                  pl.BlockSpec((None, d, ff), lambda i, e: (e, 0, 0)),
                  pl.BlockSpec((None, ff, d), lambda i, e: (e, 0, 0))],
        out_specs=row(x),
        out_shape=jax.ShapeDtypeStruct((m, d), F32),
        scratch_shapes=[pltpu.VMEM((tm, d), F32), pltpu.VMEM((tm, d), BF16),
                        pltpu.VMEM((ne, tm, LANES), F32), pltpu.VMEM((tm, d), F32)],
        compiler_params=_params("parallel", "arbitrary"),
        name="ffn_moe",
    )(x, oa, ob, oc, wo, g.reshape(1, d), wr, wg, wu, wd)


def _tile_gain(g, width):
    return jnp.tile(g.astype(F32), width // g.shape[0]).reshape(1, width)


def kernel(x_prompt, x_sample, cache_k, cache_v, state_b, w_in, w_out, norm_mix, norm_ffn,
           q_norm, k_norm, hgrn_lb, hgrn_out_norm, sgu_norm, sgu_w, sgu_b,
           ffn_w_gate, ffn_w_up, ffn_w_down, moe_router, moe_w_gate, moe_w_up, moe_w_down):
    depth = w_in.shape[0]
    bsz, seq, d = x_prompt.shape
    dbsz, dseq, _ = x_sample.shape
    past = cache_k.shape[2]
    a_heads = cache_k.shape[3]
    b_heads = state_b.shape[2]
    aw = a_heads * HEAD_DIM
    bw = b_heads * HEAD_DIM
    c_groups = sgu_w.shape[1]
    cw = c_groups * HEAD_DIM
    c_chunk = sgu_w.shape[2]

    cs = jnp.cumsum(jax.nn.softmax(hgrn_lb.astype(F32), axis=0), axis=0)
    lb_all = cs - cs[0:1]

    ck_t = jnp.transpose(cache_k, (0, 1, 3, 4, 2)).reshape(depth, dbsz, aw, past)
    cv_t = jnp.transpose(cache_v, (0, 1, 3, 4, 2)).reshape(depth, dbsz, aw, past)

    def layer(l, xp, xs, k_all, v_all):
        w_in_l = w_in[l].astype(BF16)
        w_out_l = w_out[l].astype(BF16)
        qg = _tile_gain(q_norm[l], aw)
        kg = _tile_gain(k_norm[l], aw)
        hg = _tile_gain(hgrn_out_norm[l], bw)
        sg = _tile_gain(sgu_norm[l], cw)
        lb = lb_all[l]
        lbp = jnp.stack([jnp.log(lb), jnp.log1p(-lb), 1.0 - lb])

        def sgu_params(lc):
            tril = jnp.tril(jnp.ones((lc, lc), sgu_w.dtype))
            w = (sgu_w[l][:, :lc, :lc] * tril).astype(BF16)
            bias = jnp.repeat(sgu_b[l][:, :lc].T, HEAD_DIM, axis=1)
            return w, bias

        def ffn(x, oa, ob, oc):
            i = l // 2
            if l % 2 == 0:
                return _ffn_dense(x, oa, ob, oc, w_out_l, norm_ffn[l], ffn_w_gate[i].astype(BF16),
                                  ffn_w_up[i].astype(BF16), ffn_w_down[i].astype(BF16), tm=512)
            wr = jnp.zeros((d, LANES), F32).at[:, :N_EXPERTS].set(moe_router[i])
            return _ffn_moe(x, oa, ob, oc, w_out_l, norm_ffn[l], wr, moe_w_gate[i].astype(BF16),
                            moe_w_up[i].astype(BF16), moe_w_down[i].astype(BF16), tm=512)

        x = xp.reshape(bsz * seq, d)
        q, k_all, v_all, kb, vb, bp, cp = _inproj(
            x, norm_mix[l], w_in_l, qg, kg, aw=aw, bw=bw, cw=cw, tm=512,
            key_major=(l, depth, bsz, seq, k_all, v_all))
        oa = _attn_prompt(q, kb, vb, bsz=bsz, seq=seq)
        st0 = jnp.zeros((bsz, bw // LANES, LANES, LANES), F32)
        ob, st = _hgrn(bp, lbp, hg, st0, bsz=bsz, t=seq, tc=512)
        sw, sb = sgu_params(c_chunk)
        oc, _ = _sgu(cp, sg, sw, sb, lc=c_chunk, tm=8 * c_chunk)
        yp = ffn(x, oa, ob, oc).reshape(bsz, seq, d)
        outs_p = (_tiles_to_state(st),)

        x = xs.reshape(dbsz * dseq, d)
        q, k, v, kb, vb, bp, cp = _inproj(x, norm_mix[l], w_in_l, qg, kg, aw=aw, bw=bw, cw=cw, tm=512)
        kb = jnp.swapaxes(kb.reshape(dbsz, dseq, aw), 1, 2)
        vb = jnp.swapaxes(vb.reshape(dbsz, dseq, aw), 1, 2)
        oa = _attn_sample(q, kb, vb, ck_t, cv_t, layer=l, bsz=dbsz, t=dseq, past=past)
        ob, st = _hgrn(bp, lbp, hg, _state_to_tiles(state_b[l].astype(F32)), bsz=dbsz, t=dseq, tc=dseq)
        sw, sb = sgu_params(dseq)
        oc, vn = _sgu(cp, sg, sw, sb, lc=dseq, tm=dbsz * dseq)
        ys = ffn(x, oa, ob, oc).reshape(dbsz, dseq, d)
        outs_s = (k.reshape(dbsz, dseq, a_heads, HEAD_DIM), v.reshape(dbsz, dseq, a_heads, HEAD_DIM),
                  _tiles_to_state(st), vn.reshape(dbsz, dseq, cw))
        return yp, ys, k_all, v_all, outs_p, outs_s

    yp, ys = x_prompt, x_sample
    k_all = jnp.zeros((depth, bsz, aw, seq), F32)
    v_all = jnp.zeros((depth, bsz, aw, seq), F32)
    ps, ss = [], []
    for l in range(depth):
        yp, ys, k_all, v_all, op, os_ = layer(l, yp, ys, k_all, v_all)
        ps.append(op)
        ss.append(os_)
    stack = lambda items, j: jnp.stack([it[j] for it in items])
    rows_major = lambda a: jnp.transpose(a.reshape(depth, bsz, a_heads, HEAD_DIM, seq), (0, 1, 4, 2, 3))
    return (yp, ys, rows_major(k_all), rows_major(v_all), stack(ps, 0),
            stack(ss, 0), stack(ss, 1), stack(ss, 2), stack(ss, 3))
```

```python
import functools

import jax
import jax.numpy as jnp
from jax import lax
from jax.experimental import pallas as pl
from jax.experimental.pallas import tpu as pltpu

F32 = jnp.float32
BF16 = jnp.bfloat16

HEAD_DIM = 64
LANES = 128
EPS = 1e-6
N_EXPERTS = 8
TOP_K = 2
VMEM_LIMIT = 56 * 1024 * 1024

ATT_BLOCK = 128
ATT_DEAD = -120.0
HG_STEP = 16
NEG_BIG = -1e30


def _dot(a, b):
    return jnp.dot(a, b, preferred_element_type=F32)


def _dot_nt(a, b):
    return lax.dot_general(a, b, (((1,), (1,)), ((), ())), preferred_element_type=F32)


def _split2(x):
    hi = x.astype(BF16)
    lo = (x - hi.astype(F32)).astype(BF16)
    return hi, lo


def _split3(x):
    hi = x.astype(BF16)
    r = x - hi.astype(F32)
    mid = r.astype(BF16)
    lo = (r - mid.astype(F32)).astype(BF16)
    return hi, mid, lo


def _dot_x2(x, m):
    hi, lo = _split2(x)
    return _dot(hi, m) + _dot(lo, m)


def _ldot_x3(m, x):
    hi, mid, lo = _split3(x)
    return _dot(m, hi) + _dot(m, mid) + _dot(m, lo)


def _log1p_exp_neg_abs(z):
    return jnp.log1p(jnp.exp(-jnp.abs(z)))


def _group_rmsnorm(a, ind, gain_row):
    ms = _dot_x2(a * a, ind) * (1.0 / HEAD_DIM)
    return a * lax.rsqrt(ms + EPS) * gain_row


def _group_indicator(width):
    g = jnp.arange(width, dtype=jnp.int32) // HEAD_DIM
    return (g[:, None] == g[None, :]).astype(BF16)


def _const_spec(shape):
    nd = len(shape)
    return pl.BlockSpec(shape, lambda *_: (0,) * nd)


def _params(*sem):
    return pltpu.CompilerParams(dimension_semantics=sem, vmem_limit_bytes=VMEM_LIMIT)


def _inproj_kernel(x_ref, g_ref, w_ref, qg_ref, kg_ref, ind_ref, *rest, aw, bw, key_major):
    q_ref, k_ref, v_ref, kb_ref, vb_ref, b_ref, c_ref = rest[-7:]
    x = x_ref[...]
    ms = jnp.mean(x * x, axis=-1, keepdims=True)
    xn = (x * lax.rsqrt(ms + EPS) * g_ref[...]).astype(BF16)
    ind = ind_ref[...]
    aq = _dot(xn, w_ref[:, 0:aw])
    q = _group_rmsnorm(aq, ind, qg_ref[...])
    q_ref[...] = (q * (HEAD_DIM ** -0.5)).astype(BF16)
    ak = _dot(xn, w_ref[:, aw:2 * aw])
    k = _group_rmsnorm(ak, ind, kg_ref[...])
    v = _dot(xn, w_ref[:, 2 * aw:3 * aw])
    if key_major:
        k = k.T
        v = v.T
    k_ref[...] = k
    v_ref[...] = v
    kb_ref[...] = k.astype(BF16)
    vb_ref[...] = v.astype(BF16)
    b_ref[...] = _dot(xn, w_ref[:, 3 * aw:3 * aw + 4 * bw])
    c_ref[...] = _dot(xn, w_ref[:, 3 * aw + 4 * bw:])


def _inproj(x, g, w, qg, kg, *, aw, bw, cw, tm, key_major=None):
    m, d = x.shape
    tm = min(tm, m)
    nw = w.shape[1]
    row = lambda width: pl.BlockSpec((tm, width), lambda i: (i, 0))
    in_specs = [row(d), _const_spec((1, d)), _const_spec((d, nw)),
                _const_spec((1, aw)), _const_spec((1, aw)), _const_spec((aw, aw))]
    args = [x, g.reshape(1, d), w, qg, kg, _group_indicator(aw)]
    aliases = {}
    if key_major is None:
        kv_specs = [row(aw)] * 4
        kv_shapes = [jax.ShapeDtypeStruct((m, aw), F32)] * 2 + [jax.ShapeDtypeStruct((m, aw), BF16)] * 2
    else:
        layer, depth, bsz, seq, k_all, v_all = key_major
        nt = seq // tm
        kv_specs = ([pl.BlockSpec((None, None, aw, tm), lambda i: (layer, i // nt, 0, i % nt))] * 2
                    + [pl.BlockSpec((None, aw, tm), lambda i: (i // nt, 0, i % nt))] * 2)
        kv_shapes = ([jax.ShapeDtypeStruct((depth, bsz, aw, seq), F32)] * 2
                     + [jax.ShapeDtypeStruct((bsz, aw, seq), BF16)] * 2)
        in_specs += [pl.BlockSpec(memory_space=pl.ANY)] * 2
        aliases = {len(args): 1, len(args) + 1: 2}
        args += [k_all, v_all]
    return pl.pallas_call(
        functools.partial(_inproj_kernel, aw=aw, bw=bw, key_major=key_major is not None),
        grid=(m // tm,),
        in_specs=in_specs,
        out_specs=[row(aw)] + kv_specs + [row(4 * bw), row(2 * cw)],
        out_shape=([jax.ShapeDtypeStruct((m, aw), BF16)] + kv_shapes
                   + [jax.ShapeDtypeStruct((m, 4 * bw), F32), jax.ShapeDtypeStruct((m, 2 * cw), F32)]),
        input_output_aliases=aliases,
        compiler_params=_params("parallel"),
        name="inproj",
    )(*args)


def _head_masks(rows):
    lane = lax.broadcasted_iota(jnp.int32, (rows, LANES), 1)
    return lane < HEAD_DIM


def _sb_block(qs, kblk, vblk, carry_ref, acc_ref, cum2, low, mask):
    tk = kblk.shape[1]
    rows = qs[0].shape[0] // 2
    n = 2 * rows
    npair = len(qs)
    pair = lambda p: slice(p * LANES, (p + 1) * LANES)
    zs = [_dot(q, kblk[pair(p), :]) for p, q in enumerate(qs)]
    log_betas, rs = [], []
    for p in range(npair):
        z = zs[p]
        sp = jnp.log(1.0 + jnp.exp(-jnp.abs(z)))
        log_beta = jnp.minimum(z, 0.0) - sp
        log_keep = log_beta - z
        if mask is not None:
            log_keep = jnp.where(mask, log_keep, 0.0)
        hi, lo = _split2(log_keep)
        r = _dot(jnp.concatenate([hi, lo], axis=0), cum2)
        log_betas.append(log_beta)
        rs.append(r[:n] + r[n:])
    for p in range(npair):
        carry = carry_ref[p * n:(p + 1) * n, :]
        after = rs[p][:, LANES:] + (carry if tk == LANES else carry[:, :tk])
        w = jnp.exp(log_betas[p] + after)
        if mask is not None:
            w = jnp.where(mask, w, 0.0)
        pv = _dot_nt(w.astype(BF16), vblk[pair(p), :])
        acc_ref[:, pair(p)] += jnp.where(low, pv[:rows], pv[rows:])
        carry_ref[p * n:(p + 1) * n, :] = carry + rs[p][:, :LANES]


def _stack_heads(q, low):
    qs = []
    for p in range(q.shape[1] // LANES):
        qp = q[:, p * LANES:(p + 1) * LANES]
        zero = jnp.zeros_like(qp)
        qs.append(jnp.concatenate([jnp.where(low, qp, zero), jnp.where(low, zero, qp)], axis=0))
    return qs


def _causal_mask(rows, tk, heads):
    r = lax.broadcasted_iota(jnp.int32, (heads * rows, tk), 0) & (rows - 1)
    c = lax.broadcasted_iota(jnp.int32, (heads * rows, tk), 1)
    return c < r


def _attn_prompt_kernel(q_ref, k_ref, v_ref, cum_ref, o_ref, carry_ref, acc_ref, *, blk, heads):
    i = pl.program_id(1)
    low = _head_masks(blk)
    qs = _stack_heads(q_ref[...], low)
    carry_ref[...] = jnp.zeros_like(carry_ref)
    acc_ref[...] = jnp.zeros_like(acc_ref)
    cum2 = cum_ref[...]
    start = pl.multiple_of(i * blk, blk)
    _sb_block(qs, k_ref[:, pl.ds(start, blk)], v_ref[:, pl.ds(start, blk)],
              carry_ref, acc_ref, cum2, low, _causal_mask(blk, blk, 2))

    def cond(state):
        kb, live = state
        return jnp.logical_and(kb >= 0, live > ATT_DEAD)

    def body(state):
        kb, _ = state
        s = pl.multiple_of(kb * blk, blk)
        _sb_block(qs, k_ref[:, pl.ds(s, blk)], v_ref[:, pl.ds(s, blk)],
                  carry_ref, acc_ref, cum2, low, None)
        return kb - 1, jnp.max(carry_ref[...])

    lax.while_loop(cond, body, (i - 1, jnp.max(carry_ref[...])))
    o_ref[...] = acc_ref[...].astype(o_ref.dtype)


def _cum_matrix(n):
    j = jnp.arange(n, dtype=jnp.int32)
    newer = (j[:, None] > j[None, :]).astype(BF16)
    return jnp.concatenate([jnp.ones((n, LANES), BF16), newer], axis=1)


def _attn_prompt(q, kb, vb, *, bsz, seq):
    m, aw = q.shape
    blk = ATT_BLOCK
    nq = seq // blk
    nh = aw // HEAD_DIM
    return pl.pallas_call(
        functools.partial(_attn_prompt_kernel, blk=blk, heads=nh),
        grid=(bsz, nq),
        in_specs=[pl.BlockSpec((blk, aw), lambda b, i: (b * nq + i, 0)),
                  pl.BlockSpec((None, aw, seq), lambda b, i: (b, 0, 0)),
                  pl.BlockSpec((None, aw, seq), lambda b, i: (b, 0, 0)),
                  _const_spec((blk, LANES + blk))],
        out_specs=pl.BlockSpec((blk, aw), lambda b, i: (b * nq + i, 0)),
        out_shape=jax.ShapeDtypeStruct((m, aw), BF16),
        scratch_shapes=[pltpu.VMEM((nh * blk, LANES), F32), pltpu.VMEM((blk, aw), F32)],
        compiler_params=_params("parallel", "arbitrary"),
        name="attn_prompt",
    )(q, kb, vb, _cum_matrix(blk))


def _attn_sample_kernel(q_ref, kn_ref, vn_ref, ck_hbm, cv_hbm, cum_new_ref, cum_ref,
                        o_ref, carry_ref, acc_ref, kbuf, vbuf, sem, *, layer, t, blk, past, heads):
    b = pl.program_id(0)
    nblk = past // blk

    def slot_of(kb):
        return lax.rem(nblk - 1 - kb, 2)

    def copies(kb):
        slot = slot_of(kb)
        cols = pl.ds(pl.multiple_of(kb * blk, blk), blk)
        return (pltpu.make_async_copy(ck_hbm.at[layer, b, :, cols], kbuf.at[slot], sem.at[0, slot]),
                pltpu.make_async_copy(cv_hbm.at[layer, b, :, cols], vbuf.at[slot], sem.at[1, slot]))

    def start(kb):
        for c in copies(kb):
            c.start()

    def wait(kb):
        for c in copies(kb):
            c.wait()

    start(nblk - 1)
    low = _head_masks(t)
    qs = _stack_heads(q_ref[...], low)
    carry_ref[...] = jnp.zeros_like(carry_ref)
    acc_ref[...] = jnp.zeros_like(acc_ref)
    _sb_block(qs, kn_ref[...], vn_ref[...], carry_ref, acc_ref,
              cum_new_ref[...], low, _causal_mask(t, t, 2))
    cum2 = cum_ref[...]

    def cond(state):
        kb, live = state
        return jnp.logical_and(kb >= 0, live > ATT_DEAD)

    def body(state):
        kb, _ = state
        wait(kb)

        @pl.when(kb > 0)
        def _():
            start(kb - 1)

        slot = slot_of(kb)
        _sb_block(qs, kbuf[slot].astype(BF16), vbuf[slot].astype(BF16),
                  carry_ref, acc_ref, cum2, low, None)
        return kb - 1, jnp.max(carry_ref[...])

    kb_end, _ = lax.while_loop(cond, body, (nblk - 1, jnp.max(carry_ref[...])))

    @pl.when(kb_end >= 0)
    def _():
        wait(kb_end)

    o_ref[...] = acc_ref[...].astype(o_ref.dtype)


def _attn_sample(q, kb, vb, ck, cv, *, layer, bsz, t, past):
    m, aw = q.shape
    blk = ATT_BLOCK
    nh = aw // HEAD_DIM
    assert past >= blk and past % blk == 0
    new = pl.BlockSpec((t, aw), lambda b: (b, 0))
    new_t = pl.BlockSpec((None, aw, t), lambda b: (b, 0, 0))
    hbm = pl.BlockSpec(memory_space=pl.ANY)
    return pl.pallas_call(
        functools.partial(_attn_sample_kernel, layer=layer, t=t, blk=blk, past=past, heads=nh),
        grid=(bsz,),
        in_specs=[new, new_t, new_t, hbm, hbm,
                  _const_spec((t, LANES + t)), _const_spec((blk, LANES + blk))],
        out_specs=new,
        out_shape=jax.ShapeDtypeStruct((m, aw), BF16),
        scratch_shapes=[pltpu.VMEM((nh * t, LANES), F32), pltpu.VMEM((t, aw), F32),
                        pltpu.VMEM((2, aw, blk), F32), pltpu.VMEM((2, aw, blk), F32),
                        pltpu.SemaphoreType.DMA((2, 2))],
        compiler_params=_params("arbitrary"),
        name="attn_sample",
    )(q, kb, vb, ck, cv, _cum_matrix(t), _cum_matrix(blk))


def _hgrn_group(blk, lbp, gain, tril, ind, bd_mask, st_ref):
    bw = gain.shape[1]
    n = HG_STEP
    half = n // 2
    steps = blk.shape[0] // n
    npair = bw // LANES
    pair = lambda p: slice(p * LANES, (p + 1) * LANES)
    qr = blk[:, 0:bw]
    z = blk[:, bw:2 * bw]
    v = blk[:, 2 * bw:3 * bw]
    gr = blk[:, 3 * bw:4 * bw]
    log_lb, log1m_lb, one_m_lb = lbp[0:1], lbp[1:2], lbp[2:3]

    sp = _log1p_exp_neg_abs(z)
    cc = log1m_lb + (jnp.minimum(z, 0.0) - sp)
    log_f = jnp.maximum(log_lb, cc) + jnp.log1p(jnp.exp(-jnp.abs(log_lb - cc)))
    k = one_m_lb * jnp.exp(jnp.minimum(-z, 0.0) - sp)
    q = qr * jax.nn.sigmoid(qr)
    b = _ldot_x3(tril, log_f)
    vb = v.astype(BF16)

    row = lax.broadcasted_iota(jnp.int32, (n, bw), 0)
    row_hi = lax.broadcasted_iota(jnp.int32, (half, bw), 0) + half
    parts, qbs, decays, upds = [], [], [], []
    for j in range(steps):
        sl = slice(j * n, (j + 1) * n)
        bj, qj, kj = b[sl], q[sl], k[sl]
        b_last = bj[n - 1:n]
        qbs.append((qj * jnp.exp(bj)).astype(BF16))
        kd = (kj * jnp.exp(b_last - bj)).astype(BF16)
        decays.append(jnp.exp(b_last))
        upds.append([lax.dot_general(vb[sl, pair(p)], kd[:, pair(p)], (((0,), (0,)), ((), ())),
                                     preferred_element_type=F32) for p in range(npair)])
        for s in range(half):
            e = jnp.exp(jnp.where(row >= s, bj - bj[s:s + 1], NEG_BIG))
            parts.append((qj * e) * kj[s:s + 1])
        bh, qh = bj[half:], qj[half:]
        for s in range(half, n):
            e = jnp.exp(jnp.where(row_hi >= s, bh - bj[s:s + 1], NEG_BIG))
            parts.append((qh * e) * kj[s:s + 1])
    att = _dot(jnp.concatenate(parts, axis=0).astype(BF16), ind)
    per_step = half * n + half * half

    sts = [st_ref[p] for p in range(npair)]
    outs = []
    for j in range(steps):
        base = j * per_step
        vj = v[j * n:(j + 1) * n]
        o_lo = att[base:base + n] * vj[0:1]
        for s in range(1, half):
            o_lo = o_lo + att[base + s * n:base + (s + 1) * n] * vj[s:s + 1]
        base += half * n
        o_hi = att[base:base + half] * vj[half:half + 1]
        for s in range(half + 1, n):
            o_hi = o_hi + att[base + (s - half) * half:base + (s - half + 1) * half] * vj[s:s + 1]
        o_intra = jnp.concatenate([o_lo[:half], o_lo[half:] + o_hi], axis=0)
        o_inter = []
        for p in range(npair):
            o_inter.append(_dot_nt(qbs[j][:, pair(p)], sts[p].astype(BF16)))
            sts[p] = sts[p] * decays[j][:, pair(p)] + jnp.where(bd_mask, upds[j][p], 0.0)
        outs.append(o_intra + jnp.concatenate(o_inter, axis=1))
    for p in range(npair):
        st_ref[p] = sts[p]
    o = jnp.concatenate(outs, axis=0) if steps > 1 else outs[0]

    ms = _dot_x2(o * o, ind) * (1.0 / HEAD_DIM)
    o = o * lax.rsqrt(ms + EPS) * gain
    return o * (gr * jax.nn.sigmoid(gr))


def _hgrn_kernel(b_ref, lbp_ref, gain_ref, tril_ref, ind_ref, st0_ref, o_ref, st_out_ref, st_ref,
                 *, groups, grows):
    i = pl.program_id(1)

    @pl.when(i == 0)
    def _():
        st_ref[...] = st0_ref[...]

    lbp = lbp_ref[...]
    gain = gain_ref[...]
    tril = tril_ref[...]
    ind = ind_ref[...]
    r = lax.broadcasted_iota(jnp.int32, (LANES, LANES), 0) // HEAD_DIM
    c = lax.broadcasted_iota(jnp.int32, (LANES, LANES), 1) // HEAD_DIM
    bd_mask = r == c

    def body(j, _):
        rows = pl.ds(pl.multiple_of(j * grows, grows), grows)
        o = _hgrn_group(b_ref[rows, :], lbp, gain, tril, ind, bd_mask, st_ref)
        o_ref[rows, :] = o.astype(o_ref.dtype)
        return 0

    lax.fori_loop(0, groups, body, 0)

    @pl.when(i == pl.num_programs(1) - 1)
    def _():
        st_out_ref[...] = st_ref[...]


def _hgrn(bproj, lbp, gain, st0, *, bsz, t, tc, gsteps=8):
    m, w4 = bproj.shape
    bw = w4 // 4
    tc = min(tc, t)
    nc = t // tc
    grows = min(gsteps * HG_STEP, tc)
    npair = bw // LANES
    j = jnp.arange(grows, dtype=jnp.int32)
    tril = jnp.logical_and(j[:, None] >= j[None, :],
                           j[:, None] // HG_STEP == j[None, :] // HG_STEP).astype(BF16)
    st_spec = pl.BlockSpec((None, npair, LANES, LANES), lambda b, i: (b, 0, 0, 0))
    return pl.pallas_call(
        functools.partial(_hgrn_kernel, groups=tc // grows, grows=grows),
        grid=(bsz, nc),
        in_specs=[pl.BlockSpec((tc, w4), lambda b, i: (b * nc + i, 0)),
                  _const_spec((3, bw)), _const_spec((1, bw)),
                  _const_spec((grows, grows)), _const_spec((bw, bw)), st_spec],
        out_specs=[pl.BlockSpec((tc, bw), lambda b, i: (b * nc + i, 0)), st_spec],
        out_shape=[jax.ShapeDtypeStruct((m, bw), BF16),
                   jax.ShapeDtypeStruct((bsz, npair, LANES, LANES), F32)],
        scratch_shapes=[pltpu.VMEM((npair, LANES, LANES), F32)],
        compiler_params=_params("parallel", "arbitrary"),
        name="hgrn2",
    )(bproj, lbp, gain, tril, _group_indicator(bw), st0)


def _state_to_tiles(s):
    bsz, h, dk, dv = s.shape
    st = jnp.swapaxes(s, 2, 3).reshape(bsz, h // 2, 2, dv, dk)
    z = jnp.zeros_like(st[:, :, 0])
    top = jnp.concatenate([st[:, :, 0], z], axis=-1)
    bot = jnp.concatenate([z, st[:, :, 1]], axis=-1)
    return jnp.concatenate([top, bot], axis=-2)


def _tiles_to_state(tiles):
    a = tiles[:, :, :HEAD_DIM, :HEAD_DIM]
    b = tiles[:, :, HEAD_DIM:, HEAD_DIM:]
    st = jnp.stack([a, b], axis=2)
    bsz, npair = tiles.shape[:2]
    return jnp.swapaxes(st.reshape(bsz, 2 * npair, HEAD_DIM, HEAD_DIM), 2, 3)


def _sgu_kernel(c_ref, gain_ref, w_ref, bias_ref, ind_ref, y_ref, vn_ref, *, cw, lc):
    c = c_ref[...]
    u = jax.nn.gelu(c[:, :cw])
    gv = jax.nn.gelu(c[:, cw:])
    vn = _group_rmsnorm(gv, ind_ref[...], gain_ref[...])
    vn_ref[...] = vn
    low = _head_masks(lc)
    bias = bias_ref[...]
    for j in range(c.shape[0] // lc):
        rows = slice(j * lc, (j + 1) * lc)
        mixed = []
        for p in range(cw // LANES):
            vp = vn[rows, p * LANES:(p + 1) * LANES].astype(BF16)
            mixed.append(jnp.where(low, _dot(w_ref[2 * p], vp), _dot(w_ref[2 * p + 1], vp)))
        y_ref[rows, :] = (u[rows] * (jnp.concatenate(mixed, axis=1) + bias)).astype(y_ref.dtype)


def _sgu(cproj, gain, w, bias, *, lc, tm):
    m, w2 = cproj.shape
    cw = w2 // 2
    ng = cw // HEAD_DIM
    tm = min(tm, m)
    row = lambda width: pl.BlockSpec((tm, width), lambda i: (i, 0))
    return pl.pallas_call(
        functools.partial(_sgu_kernel, cw=cw, lc=lc),
        grid=(m // tm,),
        in_specs=[row(w2), _const_spec((1, cw)), _const_spec((ng, lc, lc)),
                  _const_spec((lc, cw)), _const_spec((cw, cw))],
        out_specs=[row(cw), row(cw)],
        out_shape=[jax.ShapeDtypeStruct((m, cw), BF16), jax.ShapeDtypeStruct((m, cw), F32)],
        compiler_params=_params("parallel"),
        name="sgu",
    )(cproj, gain, w, bias, _group_indicator(cw))


def _mix_residual(x_ref, oa_ref, ob_ref, oc_ref, wo_ref):
    aw = oa_ref.shape[1]
    bw = ob_ref.shape[1]
    return (x_ref[...] + _dot(oa_ref[...], wo_ref[0:aw, :])
            + _dot(ob_ref[...], wo_ref[aw:aw + bw, :])
            + _dot(oc_ref[...], wo_ref[aw + bw:, :]))


def _rms(h, g):
    return h * lax.rsqrt(jnp.mean(h * h, axis=-1, keepdims=True) + EPS) * g


def _ffn_dense_kernel(x_ref, oa_ref, ob_ref, oc_ref, wo_ref, g_ref, wg_ref, wu_ref, wd_ref, y_ref,
                      *, fchunk):
    h = _mix_residual(x_ref, oa_ref, ob_ref, oc_ref, wo_ref)
    hn = _rms(h, g_ref[...]).astype(BF16)
    y_ref[...] = h
    for f0 in range(0, wg_ref.shape[1], fchunk):
        cols = slice(f0, f0 + fchunk)
        gate = _dot(hn, wg_ref[:, cols])
        up = _dot(hn, wu_ref[:, cols])
        act = (gate * jax.nn.sigmoid(gate) * up).astype(BF16)
        y_ref[...] += _dot(act, wd_ref[cols, :])


def _ffn_dense(x, oa, ob, oc, wo, g, wg, wu, wd, *, tm):
    m, d = x.shape
    tm = min(tm, m)
    ff = wg.shape[1]
    fchunk = ff // 2 if (ff // 2) % LANES == 0 else ff
    row = lambda a: pl.BlockSpec((tm, a.shape[1]), lambda i: (i, 0))
    return pl.pallas_call(
        functools.partial(_ffn_dense_kernel, fchunk=fchunk),
        grid=(m // tm,),
        in_specs=[row(x), row(oa), row(ob), row(oc), _const_spec(wo.shape), _const_spec((1, d)),
                  _const_spec(wg.shape), _const_spec(wu.shape), _const_spec(wd.shape)],
        out_specs=row(x),
        out_shape=jax.ShapeDtypeStruct((m, d), F32),
        compiler_params=_params("parallel"),
        name="ffn_dense",
    )(x, oa, ob, oc, wo, g.reshape(1, d), wg, wu, wd)


def _route(hn, wr):
    logits = jnp.dot(hn, wr, preferred_element_type=F32, precision=lax.Precision.HIGHEST)
    lane = lax.broadcasted_iota(jnp.int32, logits.shape, 1).astype(F32)
    logits = jnp.where(lane < N_EXPERTS, logits, -jnp.inf)
    m1 = jnp.max(logits, axis=-1, keepdims=True)
    i1 = jnp.min(jnp.where(logits == m1, lane, float(LANES)), axis=-1, keepdims=True)
    rest = jnp.where(lane == i1, -jnp.inf, logits)
    m2 = jnp.max(rest, axis=-1, keepdims=True)
    i2 = jnp.min(jnp.where(rest == m2, lane, float(LANES)), axis=-1, keepdims=True)
    e2 = jnp.exp(m2 - m1)
    g1 = 1.0 / (1.0 + e2)
    g2 = e2 / (1.0 + e2)
    return jnp.where(lane == i1, g1, 0.0) + jnp.where(lane == i2, g2, 0.0)


def _ffn_moe_kernel(x_ref, oa_ref, ob_ref, oc_ref, wo_ref, g_ref, wr_ref, wg_ref, wu_ref, wd_ref,
                    y_ref, h_ref, hn_ref, comb_ref, acc_ref):
    e = pl.program_id(1)

    @pl.when(e == 0)
    def _():
        h = _mix_residual(x_ref, oa_ref, ob_ref, oc_ref, wo_ref)
        hn = _rms(h, g_ref[...])
        h_ref[...] = h
        hn_ref[...] = hn.astype(BF16)
        comb = _route(hn, wr_ref[...])
        for j in range(N_EXPERTS):
            comb_ref[j] = jnp.broadcast_to(comb[:, j:j + 1], comb.shape)
        acc_ref[...] = jnp.zeros_like(acc_ref)

    hn = hn_ref[...]
    gate = _dot(hn, wg_ref[...])
    up = _dot(hn, wu_ref[...])
    act = (gate * jax.nn.sigmoid(gate) * up).astype(BF16)
    out = _dot(act, wd_ref[...])
    cb = comb_ref[e]
    for j in range(out.shape[1] // LANES):
        lanes = slice(j * LANES, (j + 1) * LANES)
        acc_ref[:, lanes] += cb * out[:, lanes]

    @pl.when(e == pl.num_programs(1) - 1)
    def _():
        y_ref[...] = h_ref[...] + acc_ref[...]


def _ffn_moe(x, oa, ob, oc, wo, g, wr, wg, wu, wd, *, tm):
    m, d = x.shape
    tm = min(tm, m)
    ne, _, ff = wg.shape
    row = lambda a: pl.BlockSpec((tm, a.shape[1]), lambda i, e: (i, 0))
    const = lambda shape: pl.BlockSpec(shape, lambda i, e: (0,) * len(shape))
    return pl.pallas_call(
        _ffn_moe_kernel,
        grid=(m // tm, ne),
        in_specs=[row(x), row(oa), row(ob), row(oc), const(wo.shape), const((1, d)),
                  const(wr.shape),
                  pl.BlockSpec((None, d, ff), lambda i, e: (e, 0, 0)),
                  pl.BlockSpec((None, d, ff), lambda i, e: (e, 0, 0)),
                  pl.BlockSpec((None, ff, d), lambda i, e: (e, 0, 0))],
        out_specs=row(x),
        out_shape=jax.ShapeDtypeStruct((m, d), F32),
        scratch_shapes=[pltpu.VMEM((tm, d), F32), pltpu.VMEM((tm, d), BF16),
                        pltpu.VMEM((ne, tm, LANES), F32), pltpu.VMEM((tm, d), F32)],
        compiler_params=_params("parallel", "arbitrary"),
        name="ffn_moe",
    )(x, oa, ob, oc, wo, g.reshape(1, d), wr, wg, wu, wd)


def _tile_gain(g, width):
    return jnp.tile(g.astype(F32), width // g.shape[0]).reshape(1, width)


def kernel(x_prompt, x_sample, cache_k, cache_v, state_b, w_in, w_out, norm_mix, norm_ffn,
           q_norm, k_norm, hgrn_lb, hgrn_out_norm, sgu_norm, sgu_w, sgu_b,
           ffn_w_gate, ffn_w_up, ffn_w_down, moe_router, moe_w_gate, moe_w_up, moe_w_down):
    depth = w_in.shape[0]
    bsz, seq, d = x_prompt.shape
    dbsz, dseq, _ = x_sample.shape
    past = cache_k.shape[2]
    a_heads = cache_k.shape[3]
    b_heads = state_b.shape[2]
    aw = a_heads * HEAD_DIM
    bw = b_heads * HEAD_DIM
    c_groups = sgu_w.shape[1]
    cw = c_groups * HEAD_DIM
    c_chunk = sgu_w.shape[2]

    cs = jnp.cumsum(jax.nn.softmax(hgrn_lb.astype(F32), axis=0), axis=0)
    lb_all = cs - cs[0:1]

    ck_t = jnp.transpose(cache_k, (0, 1, 3, 4, 2)).reshape(depth, dbsz, aw, past)
    cv_t = jnp.transpose(cache_v, (0, 1, 3, 4, 2)).reshape(depth, dbsz, aw, past)

    def layer(l, xp, xs, k_all, v_all):
        w_in_l = w_in[l].astype(BF16)
        w_out_l = w_out[l].astype(BF16)
        qg = _tile_gain(q_norm[l], aw)
        kg = _tile_gain(k_norm[l], aw)
        hg = _tile_gain(hgrn_out_norm[l], bw)
        sg = _tile_gain(sgu_norm[l], cw)
        lb = lb_all[l]
        lbp = jnp.stack([jnp.log(lb), jnp.log1p(-lb), 1.0 - lb])

        def sgu_params(lc):
            tril = jnp.tril(jnp.ones((lc, lc), sgu_w.dtype))
            w = (sgu_w[l][:, :lc, :lc] * tril).astype(BF16)
            bias = jnp.repeat(sgu_b[l][:, :lc].T, HEAD_DIM, axis=1)
            return w, bias

        def ffn(x, oa, ob, oc):
            i = l // 2
            if l % 2 == 0:
                return _ffn_dense(x, oa, ob, oc, w_out_l, norm_ffn[l], ffn_w_gate[i].astype(BF16),
                                  ffn_w_up[i].astype(BF16), ffn_w_down[i].astype(BF16), tm=512)
            wr = jnp.zeros((d, LANES), F32).at[:, :N_EXPERTS].set(moe_router[i])
            return _ffn_moe(x, oa, ob, oc, w_out_l, norm_ffn[l], wr, moe_w_gate[i].astype(BF16),
                            moe_w_up[i].astype(BF16), moe_w_down[i].astype(BF16), tm=512)

        x = xp.reshape(bsz * seq, d)
        q, k_all, v_all, kb, vb, bp, cp = _inproj(
            x, norm_mix[l], w_in_l, qg, kg, aw=aw, bw=bw, cw=cw, tm=512,
            key_major=(l, depth, bsz, seq, k_all, v_all))
        oa = _attn_prompt(q, kb, vb, bsz=bsz, seq=seq)
        st0 = jnp.zeros((bsz, bw // LANES, LANES, LANES), F32)
        ob, st = _hgrn(bp, lbp, hg, st0, bsz=bsz, t=seq, tc=512)
        sw, sb = sgu_params(c_chunk)
        oc, _ = _sgu(cp, sg, sw, sb, lc=c_chunk, tm=8 * c_chunk)
        yp = ffn(x, oa, ob, oc).reshape(bsz, seq, d)
        outs_p = (_tiles_to_state(st),)

        x = xs.reshape(dbsz * dseq, d)
        q, k, v, kb, vb, bp, cp = _inproj(x, norm_mix[l], w_in_l, qg, kg, aw=aw, bw=bw, cw=cw, tm=512)
        kb = jnp.swapaxes(kb.reshape(dbsz, dseq, aw), 1, 2)
        vb = jnp.swapaxes(vb.reshape(dbsz, dseq, aw), 1, 2)
        oa = _attn_sample(q, kb, vb, ck_t, cv_t, layer=l, bsz=dbsz, t=dseq, past=past)
        ob, st = _hgrn(bp, lbp, hg, _state_to_tiles(state_b[l].astype(F32)), bsz=dbsz, t=dseq, tc=dseq)
        sw, sb = sgu_params(dseq)
        oc, vn = _sgu(cp, sg, sw, sb, lc=dseq, tm=dbsz * dseq)
        ys = ffn(x, oa, ob, oc).reshape(dbsz, dseq, d)
        outs_s = (k.reshape(dbsz, dseq, a_heads, HEAD_DIM), v.reshape(dbsz, dseq, a_heads, HEAD_DIM),
                  _tiles_to_state(st), vn.reshape(dbsz, dseq, cw))
        return yp, ys, k_all, v_all, outs_p, outs_s

    yp, ys = x_prompt, x_sample
    k_all = jnp.zeros((depth, bsz, aw, seq), F32)
    v_all = jnp.zeros((depth, bsz, aw, seq), F32)
    ps, ss = [], []
    for l in range(depth):
        yp, ys, k_all, v_all, op, os_ = layer(l, yp, ys, k_all, v_all)
        ps.append(op)
        ss.append(os_)
    stack = lambda items, j: jnp.stack([it[j] for it in items])
    rows_major = lambda a: jnp.transpose(a.reshape(depth, bsz, a_heads, HEAD_DIM, seq), (0, 1, 4, 2, 3))
    return (yp, ys, rows_major(k_all), rows_major(v_all), stack(ps, 0),
            stack(ss, 0), stack(ss, 1), stack(ss, 2), stack(ss, 3))
```

```python
import functools

import jax
import jax.numpy as jnp
from jax import lax
from jax.experimental import pallas as pl
from jax.experimental.pallas import tpu as pltpu

F32 = jnp.float32
BF16 = jnp.bfloat16

HEAD_DIM = 64
LANES = 128
EPS = 1e-6
N_EXPERTS = 8
TOP_K = 2
VMEM_LIMIT = 56 * 1024 * 1024

ATT_BLOCK = 128
ATT_DEAD = -120.0
HG_STEP = 16
NEG_BIG = -1e30


def _dot(a, b):
    return jnp.dot(a, b, preferred_element_type=F32)


def _dot_nt(a, b):
    return lax.dot_general(a, b, (((1,), (1,)), ((), ())), preferred_element_type=F32)


def _split2(x):
    hi = x.astype(BF16)
    lo = (x - hi.astype(F32)).astype(BF16)
    return hi, lo


def _split3(x):
    hi = x.astype(BF16)
    r = x - hi.astype(F32)
    mid = r.astype(BF16)
    lo = (r - mid.astype(F32)).astype(BF16)
    return hi, mid, lo


def _dot_x2(x, m):
    hi, lo = _split2(x)
    return _dot(hi, m) + _dot(lo, m)


def _ldot_x3(m, x):
    hi, mid, lo = _split3(x)
    return _dot(m, hi) + _dot(m, mid) + _dot(m, lo)


def _softplus_neg_abs(z):
    return jnp.log(1.0 + jnp.exp(-jnp.abs(z)))


def _group_rmsnorm(a, ind, gain_row):
    ms = _dot_x2(a * a, ind) * (1.0 / HEAD_DIM)
    return a * lax.rsqrt(ms + EPS) * gain_row


def _group_indicator(width):
    g = jnp.arange(width, dtype=jnp.int32) // HEAD_DIM
    return (g[:, None] == g[None, :]).astype(BF16)


def _const_spec(shape):
    nd = len(shape)
    return pl.BlockSpec(shape, lambda *_: (0,) * nd)


def _params(*sem):
    return pltpu.CompilerParams(dimension_semantics=sem, vmem_limit_bytes=VMEM_LIMIT)


def _inproj_kernel(x_ref, g_ref, w_ref, qg_ref, kg_ref, ind_ref, *rest, aw, bw, key_major):
    q_ref, k_ref, v_ref, kb_ref, vb_ref, b_ref, c_ref = rest[-7:]
    x = x_ref[...]
    ms = jnp.mean(x * x, axis=-1, keepdims=True)
    xn = (x * lax.rsqrt(ms + EPS) * g_ref[...]).astype(BF16)
    ind = ind_ref[...]
    aq = _dot(xn, w_ref[:, 0:aw])
    q = _group_rmsnorm(aq, ind, qg_ref[...])
    q_ref[...] = (q * (HEAD_DIM ** -0.5)).astype(BF16)
    ak = _dot(xn, w_ref[:, aw:2 * aw])
    k = _group_rmsnorm(ak, ind, kg_ref[...])
    v = _dot(xn, w_ref[:, 2 * aw:3 * aw])
    if key_major:
        k = k.T
        v = v.T
    k_ref[...] = k
    v_ref[...] = v
    kb_ref[...] = k.astype(BF16)
    vb_ref[...] = v.astype(BF16)
    b_ref[...] = _dot(xn, w_ref[:, 3 * aw:3 * aw + 4 * bw])
    c_ref[...] = _dot(xn, w_ref[:, 3 * aw + 4 * bw:])


def _inproj(x, g, w, qg, kg, *, aw, bw, cw, tm, key_major=None):
    m, d = x.shape
    tm = min(tm, m)
    nw = w.shape[1]
    row = lambda width: pl.BlockSpec((tm, width), lambda i: (i, 0))
    in_specs = [row(d), _const_spec((1, d)), _const_spec((d, nw)),
                _const_spec((1, aw)), _const_spec((1, aw)), _const_spec((aw, aw))]
    args = [x, g.reshape(1, d), w, qg, kg, _group_indicator(aw)]
    aliases = {}
    if key_major is None:
        kv_specs = [row(aw)] * 4
        kv_shapes = [jax.ShapeDtypeStruct((m, aw), F32)] * 2 + [jax.ShapeDtypeStruct((m, aw), BF16)] * 2
    else:
        layer, depth, bsz, seq, k_all, v_all = key_major
        nt = seq // tm
        kv_specs = ([pl.BlockSpec((None, None, aw, tm), lambda i: (layer, i // nt, 0, i % nt))] * 2
                    + [pl.BlockSpec((None, aw, tm), lambda i: (i // nt, 0, i % nt))] * 2)
        kv_shapes = ([jax.ShapeDtypeStruct((depth, bsz, aw, seq), F32)] * 2
                     + [jax.ShapeDtypeStruct((bsz, aw, seq), BF16)] * 2)
        in_specs += [pl.BlockSpec(memory_space=pl.ANY)] * 2
        aliases = {len(args): 1, len(args) + 1: 2}
        args += [k_all, v_all]
    return pl.pallas_call(
        functools.partial(_inproj_kernel, aw=aw, bw=bw, key_major=key_major is not None),
        grid=(m // tm,),
        in_specs=in_specs,
        out_specs=[row(aw)] + kv_specs + [row(4 * bw), row(2 * cw)],
        out_shape=([jax.ShapeDtypeStruct((m, aw), BF16)] + kv_shapes
                   + [jax.ShapeDtypeStruct((m, 4 * bw), F32), jax.ShapeDtypeStruct((m, 2 * cw), F32)]),
        input_output_aliases=aliases,
        compiler_params=_params("parallel"),
        name="inproj",
    )(*args)


def _head_masks(rows):
    lane = lax.broadcasted_iota(jnp.int32, (rows, LANES), 1)
    return lane < HEAD_DIM


def _sb_block(qs, kblk, vblk, carry_ref, acc_ref, cum2, low, mask):
    tk = kblk.shape[1]
    rows = qs[0].shape[0] // 2
    n = 2 * rows
    npair = len(qs)
    pair = lambda p: slice(p * LANES, (p + 1) * LANES)
    zs = [_dot(q, kblk[pair(p), :]) for p, q in enumerate(qs)]
    log_betas, rs = [], []
    for p in range(npair):
        z = zs[p]
        sp = jnp.log(1.0 + jnp.exp(-jnp.abs(z)))
        log_beta = jnp.minimum(z, 0.0) - sp
        log_keep = log_beta - z
        if mask is not None:
            log_keep = jnp.where(mask, log_keep, 0.0)
        hi, lo = _split2(log_keep)
        r = _dot(jnp.concatenate([hi, lo], axis=0), cum2)
        log_betas.append(log_beta)
        rs.append(r[:n] + r[n:])
    for p in range(npair):
        carry = carry_ref[p * n:(p + 1) * n, :]
        after = rs[p][:, LANES:] + (carry if tk == LANES else carry[:, :tk])
        w = jnp.exp(log_betas[p] + after)
        if mask is not None:
            w = jnp.where(mask, w, 0.0)
        pv = _dot_nt(w.astype(BF16), vblk[pair(p), :])
        acc_ref[:, pair(p)] += jnp.where(low, pv[:rows], pv[rows:])
        carry_ref[p * n:(p + 1) * n, :] = carry + rs[p][:, :LANES]


def _stack_heads(q, low):
    qs = []
    for p in range(q.shape[1] // LANES):
        qp = q[:, p * LANES:(p + 1) * LANES]
        zero = jnp.zeros_like(qp)
        qs.append(jnp.concatenate([jnp.where(low, qp, zero), jnp.where(low, zero, qp)], axis=0))
    return qs


def _causal_mask(rows, tk, heads):
    r = lax.broadcasted_iota(jnp.int32, (heads * rows, tk), 0) & (rows - 1)
    c = lax.broadcasted_iota(jnp.int32, (heads * rows, tk), 1)
    return c < r


def _attn_prompt_kernel(q_ref, k_ref, v_ref, cum_ref, o_ref, carry_ref, acc_ref, *, blk, heads):
    i = pl.program_id(1)
    low = _head_masks(blk)
    qs = _stack_heads(q_ref[...], low)
    carry_ref[...] = jnp.zeros_like(carry_ref)
    acc_ref[...] = jnp.zeros_like(acc_ref)
    cum2 = cum_ref[...]
    start = pl.multiple_of(i * blk, blk)
    _sb_block(qs, k_ref[:, pl.ds(start, blk)], v_ref[:, pl.ds(start, blk)],
              carry_ref, acc_ref, cum2, low, _causal_mask(blk, blk, 2))

    def older_block(kb):
        s = pl.multiple_of(kb * blk, blk)
        _sb_block(qs, k_ref[:, pl.ds(s, blk)], v_ref[:, pl.ds(s, blk)],
                  carry_ref, acc_ref, cum2, low, None)

    def cond(state):
        kb, live = state
        return jnp.logical_and(kb >= 1, live > ATT_DEAD)

    def body(state):
        kb, _ = state
        older_block(kb)
        older_block(kb - 1)
        return kb - 2, jnp.max(carry_ref[...])

    kb_end, live_end = lax.while_loop(cond, body, (i - 1, jnp.max(carry_ref[...])))

    @pl.when(jnp.logical_and(kb_end == 0, live_end > ATT_DEAD))
    def _():
        older_block(0)

    o_ref[...] = acc_ref[...].astype(o_ref.dtype)


def _cum_matrix(n):
    j = jnp.arange(n, dtype=jnp.int32)
    newer = (j[:, None] > j[None, :]).astype(BF16)
    return jnp.concatenate([jnp.ones((n, LANES), BF16), newer], axis=1)


def _attn_prompt(q, kb, vb, *, bsz, seq):
    m, aw = q.shape
    blk = ATT_BLOCK
    nq = seq // blk
    nh = aw // HEAD_DIM
    return pl.pallas_call(
        functools.partial(_attn_prompt_kernel, blk=blk, heads=nh),
        grid=(bsz, nq),
        in_specs=[pl.BlockSpec((blk, aw), lambda b, i: (b * nq + i, 0)),
                  pl.BlockSpec((None, aw, seq), lambda b, i: (b, 0, 0)),
                  pl.BlockSpec((None, aw, seq), lambda b, i: (b, 0, 0)),
                  _const_spec((blk, LANES + blk))],
        out_specs=pl.BlockSpec((blk, aw), lambda b, i: (b * nq + i, 0)),
        out_shape=jax.ShapeDtypeStruct((m, aw), BF16),
        scratch_shapes=[pltpu.VMEM((nh * blk, LANES), F32), pltpu.VMEM((blk, aw), F32)],
        compiler_params=_params("parallel", "arbitrary"),
        name="attn_prompt",
    )(q, kb, vb, _cum_matrix(blk))


def _attn_sample_kernel(q_ref, kn_ref, vn_ref, ck_hbm, cv_hbm, cum_new_ref, cum_ref,
                        o_ref, carry_ref, acc_ref, kbuf, vbuf, sem, *, layer, t, blk, past, heads):
    b = pl.program_id(0)
    nblk = past // blk

    def slot_of(kb):
        return lax.rem(nblk - 1 - kb, 2)

    def copies(kb):
        slot = slot_of(kb)
        cols = pl.ds(pl.multiple_of(kb * blk, blk), blk)
        return (pltpu.make_async_copy(ck_hbm.at[layer, b, :, cols], kbuf.at[slot], sem.at[0, slot]),
                pltpu.make_async_copy(cv_hbm.at[layer, b, :, cols], vbuf.at[slot], sem.at[1, slot]))

    def start(kb):
        for c in copies(kb):
            c.start()

    def wait(kb):
        for c in copies(kb):
            c.wait()

    start(nblk - 1)
    low = _head_masks(t)
    qs = _stack_heads(q_ref[...], low)
    carry_ref[...] = jnp.zeros_like(carry_ref)
    acc_ref[...] = jnp.zeros_like(acc_ref)
    _sb_block(qs, kn_ref[...], vn_ref[...], carry_ref, acc_ref,
              cum_new_ref[...], low, _causal_mask(t, t, 2))
    cum2 = cum_ref[...]

    def cond(state):
        kb, live = state
        return jnp.logical_and(kb >= 0, live > ATT_DEAD)

    def body(state):
        kb, _ = state
        wait(kb)

        @pl.when(kb > 0)
        def _():
            start(kb - 1)

        slot = slot_of(kb)
        _sb_block(qs, kbuf[slot].astype(BF16), vbuf[slot].astype(BF16),
                  carry_ref, acc_ref, cum2, low, None)
        return kb - 1, jnp.max(carry_ref[...])

    kb_end, _ = lax.while_loop(cond, body, (nblk - 1, jnp.max(carry_ref[...])))

    @pl.when(kb_end >= 0)
    def _():
        wait(kb_end)

    o_ref[...] = acc_ref[...].astype(o_ref.dtype)


def _attn_sample(q, kb, vb, ck, cv, *, layer, bsz, t, past):
    m, aw = q.shape
    blk = ATT_BLOCK
    nh = aw // HEAD_DIM
    assert past >= blk and past % blk == 0
    new = pl.BlockSpec((t, aw), lambda b: (b, 0))
    new_t = pl.BlockSpec((None, aw, t), lambda b: (b, 0, 0))
    hbm = pl.BlockSpec(memory_space=pl.ANY)
    return pl.pallas_call(
        functools.partial(_attn_sample_kernel, layer=layer, t=t, blk=blk, past=past, heads=nh),
        grid=(bsz,),
        in_specs=[new, new_t, new_t, hbm, hbm,
                  _const_spec((t, LANES + t)), _const_spec((blk, LANES + blk))],
        out_specs=new,
        out_shape=jax.ShapeDtypeStruct((m, aw), BF16),
        scratch_shapes=[pltpu.VMEM((nh * t, LANES), F32), pltpu.VMEM((t, aw), F32),
                        pltpu.VMEM((2, aw, blk), F32), pltpu.VMEM((2, aw, blk), F32),
                        pltpu.SemaphoreType.DMA((2, 2))],
        compiler_params=_params("arbitrary"),
        name="attn_sample",
    )(q, kb, vb, ck, cv, _cum_matrix(t), _cum_matrix(blk))


def _hgrn_group(blk, lbp, gain, tril, ind, bd_mask, st_ref):
    bw = gain.shape[1]
    n = HG_STEP
    half = n // 2
    steps = blk.shape[0] // n
    npair = bw // LANES
    pair = lambda p: slice(p * LANES, (p + 1) * LANES)
    qr = blk[:, 0:bw]
    z = blk[:, bw:2 * bw]
    v = blk[:, 2 * bw:3 * bw]
    gr = blk[:, 3 * bw:4 * bw]
    log_lb, log1m_lb, one_m_lb = lbp[0:1], lbp[1:2], lbp[2:3]

    sp = _softplus_neg_abs(z)
    cc = log1m_lb + (jnp.minimum(z, 0.0) - sp)
    log_f = jnp.maximum(log_lb, cc) + _softplus_neg_abs(log_lb - cc)
    k = one_m_lb * jnp.exp(jnp.minimum(-z, 0.0) - sp)
    q = qr * jax.nn.sigmoid(qr)
    b = _ldot_x3(tril, log_f)
    vb = v.astype(BF16)

    row = lax.broadcasted_iota(jnp.int32, (n, bw), 0)
    row_hi = lax.broadcasted_iota(jnp.int32, (half, bw), 0) + half
    parts, qbs, decays, upds = [], [], [], []
    for j in range(steps):
        sl = slice(j * n, (j + 1) * n)
        bj, qj, kj = b[sl], q[sl], k[sl]
        b_last = bj[n - 1:n]
        qbs.append((qj * jnp.exp(bj)).astype(BF16))
        kd = (kj * jnp.exp(b_last - bj)).astype(BF16)
        decays.append(jnp.exp(b_last))
        upds.append([lax.dot_general(vb[sl, pair(p)], kd[:, pair(p)], (((0,), (0,)), ((), ())),
                                     preferred_element_type=F32) for p in range(npair)])
        for s in range(half):
            e = jnp.exp(jnp.where(row >= s, bj - bj[s:s + 1], NEG_BIG))
            parts.append((qj * e) * kj[s:s + 1])
        bh, qh = bj[half:], qj[half:]
        for s in range(half, n):
            e = jnp.exp(jnp.where(row_hi >= s, bh - bj[s:s + 1], NEG_BIG))
            parts.append((qh * e) * kj[s:s + 1])
    att = _dot(jnp.concatenate(parts, axis=0).astype(BF16), ind)
    per_step = half * n + half * half

    sts = [st_ref[p] for p in range(npair)]
    outs = []
    for j in range(steps):
        base = j * per_step
        vj = v[j * n:(j + 1) * n]
        o_lo = att[base:base + n] * vj[0:1]
        for s in range(1, half):
            o_lo = o_lo + att[base + s * n:base + (s + 1) * n] * vj[s:s + 1]
        base += half * n
        o_hi = att[base:base + half] * vj[half:half + 1]
        for s in range(half + 1, n):
            o_hi = o_hi + att[base + (s - half) * half:base + (s - half + 1) * half] * vj[s:s + 1]
        o_intra = jnp.concatenate([o_lo[:half], o_lo[half:] + o_hi], axis=0)
        o_inter = []
        for p in range(npair):
            o_inter.append(_dot_nt(qbs[j][:, pair(p)], sts[p].astype(BF16)))
            sts[p] = sts[p] * decays[j][:, pair(p)] + jnp.where(bd_mask, upds[j][p], 0.0)
        outs.append(o_intra + jnp.concatenate(o_inter, axis=1))
    for p in range(npair):
        st_ref[p] = sts[p]
    o = jnp.concatenate(outs, axis=0) if steps > 1 else outs[0]

    ms = _dot_x2(o * o, ind) * (1.0 / HEAD_DIM)
    o = o * lax.rsqrt(ms + EPS) * gain
    return o * (gr * jax.nn.sigmoid(gr))


def _hgrn_kernel(b_ref, lbp_ref, gain_ref, tril_ref, ind_ref, st0_ref, o_ref, st_out_ref, st_ref,
                 *, groups, grows):
    i = pl.program_id(1)

    @pl.when(i == 0)
    def _():
        st_ref[...] = st0_ref[...]

    lbp = lbp_ref[...]
    gain = gain_ref[...]
    tril = tril_ref[...]
    ind = ind_ref[...]
    r = lax.broadcasted_iota(jnp.int32, (LANES, LANES), 0) // HEAD_DIM
    c = lax.broadcasted_iota(jnp.int32, (LANES, LANES), 1) // HEAD_DIM
    bd_mask = r == c

    def body(j, _):
        rows = pl.ds(pl.multiple_of(j * grows, grows), grows)
        o = _hgrn_group(b_ref[rows, :], lbp, gain, tril, ind, bd_mask, st_ref)
        o_ref[rows, :] = o.astype(o_ref.dtype)
        return 0

    lax.fori_loop(0, groups, body, 0)

    @pl.when(i == pl.num_programs(1) - 1)
    def _():
        st_out_ref[...] = st_ref[...]


def _hgrn(bproj, lbp, gain, st0, *, bsz, t, tc, gsteps=16):
    m, w4 = bproj.shape
    bw = w4 // 4
    tc = min(tc, t)
    nc = t // tc
    grows = min(gsteps * HG_STEP, tc)
    npair = bw // LANES
    j = jnp.arange(grows, dtype=jnp.int32)
    tril = jnp.logical_and(j[:, None] >= j[None, :],
                           j[:, None] // HG_STEP == j[None, :] // HG_STEP).astype(BF16)
    st_spec = pl.BlockSpec((None, npair, LANES, LANES), lambda b, i: (b, 0, 0, 0))
    return pl.pallas_call(
        functools.partial(_hgrn_kernel, groups=tc // grows, grows=grows),
        grid=(bsz, nc),
        in_specs=[pl.BlockSpec((tc, w4), lambda b, i: (b * nc + i, 0)),
                  _const_spec((3, bw)), _const_spec((1, bw)),
                  _const_spec((grows, grows)), _const_spec((bw, bw)), st_spec],
        out_specs=[pl.BlockSpec((tc, bw), lambda b, i: (b * nc + i, 0)), st_spec],
        out_shape=[jax.ShapeDtypeStruct((m, bw), BF16),
                   jax.ShapeDtypeStruct((bsz, npair, LANES, LANES), F32)],
        scratch_shapes=[pltpu.VMEM((npair, LANES, LANES), F32)],
        compiler_params=_params("parallel", "arbitrary"),
        name="hgrn2",
    )(bproj, lbp, gain, tril, _group_indicator(bw), st0)


def _state_to_tiles(s):
    bsz, h, dk, dv = s.shape
    st = jnp.swapaxes(s, 2, 3).reshape(bsz, h // 2, 2, dv, dk)
    z = jnp.zeros_like(st[:, :, 0])
    top = jnp.concatenate([st[:, :, 0], z], axis=-1)
    bot = jnp.concatenate([z, st[:, :, 1]], axis=-1)
    return jnp.concatenate([top, bot], axis=-2)


def _tiles_to_state(tiles):
    a = tiles[:, :, :HEAD_DIM, :HEAD_DIM]
    b = tiles[:, :, HEAD_DIM:, HEAD_DIM:]
    st = jnp.stack([a, b], axis=2)
    bsz, npair = tiles.shape[:2]
    return jnp.swapaxes(st.reshape(bsz, 2 * npair, HEAD_DIM, HEAD_DIM), 2, 3)


def _sgu_kernel(c_ref, gain_ref, w_ref, bias_ref, ind_ref, y_ref, vn_ref, *, cw, lc):
    c = c_ref[...]
    u = jax.nn.gelu(c[:, :cw])
    gv = jax.nn.gelu(c[:, cw:])
    vn = _group_rmsnorm(gv, ind_ref[...], gain_ref[...])
    vn_ref[...] = vn
    low = _head_masks(lc)
    bias = bias_ref[...]
    for j in range(c.shape[0] // lc):
        rows = slice(j * lc, (j + 1) * lc)
        mixed = []
        for p in range(cw // LANES):
            vp = vn[rows, p * LANES:(p + 1) * LANES].astype(BF16)
            mixed.append(jnp.where(low, _dot(w_ref[2 * p], vp), _dot(w_ref[2 * p + 1], vp)))
        y_ref[rows, :] = (u[rows] * (jnp.concatenate(mixed, axis=1) + bias)).astype(y_ref.dtype)


def _sgu(cproj, gain, w, bias, *, lc, tm):
    m, w2 = cproj.shape
    cw = w2 // 2
    ng = cw // HEAD_DIM
    tm = min(tm, m)
    row = lambda width: pl.BlockSpec((tm, width), lambda i: (i, 0))
    return pl.pallas_call(
        functools.partial(_sgu_kernel, cw=cw, lc=lc),
        grid=(m // tm,),
        in_specs=[row(w2), _const_spec((1, cw)), _const_spec((ng, lc, lc)),
                  _const_spec((lc, cw)), _const_spec((cw, cw))],
        out_specs=[row(cw), row(cw)],
        out_shape=[jax.ShapeDtypeStruct((m, cw), BF16), jax.ShapeDtypeStruct((m, cw), F32)],
        compiler_params=_params("parallel"),
        name="sgu",
    )(cproj, gain, w, bias, _group_indicator(cw))


def _mix_residual(x_ref, oa_ref, ob_ref, oc_ref, wo_ref):
    aw = oa_ref.shape[1]
    bw = ob_ref.shape[1]
    return (x_ref[...] + _dot(oa_ref[...], wo_ref[0:aw, :])
            + _dot(ob_ref[...], wo_ref[aw:aw + bw, :])
            + _dot(oc_ref[...], wo_ref[aw + bw:, :]))


def _rms(h, g):
    return h * lax.rsqrt(jnp.mean(h * h, axis=-1, keepdims=True) + EPS) * g


def _ffn_dense_kernel(x_ref, oa_ref, ob_ref, oc_ref, wo_ref, g_ref, wg_ref, wu_ref, wd_ref, y_ref,
                      *, fchunk):
    h = _mix_residual(x_ref, oa_ref, ob_ref, oc_ref, wo_ref)
    hn = _rms(h, g_ref[...]).astype(BF16)
    y_ref[...] = h
    for f0 in range(0, wg_ref.shape[1], fchunk):
        cols = slice(f0, f0 + fchunk)
        gate = _dot(hn, wg_ref[:, cols])
        up = _dot(hn, wu_ref[:, cols])
        act = (gate * jax.nn.sigmoid(gate) * up).astype(BF16)
        y_ref[...] += _dot(act, wd_ref[cols, :])


def _ffn_dense(x, oa, ob, oc, wo, g, wg, wu, wd, *, tm):
    m, d = x.shape
    tm = min(tm, m)
    ff = wg.shape[1]
    fchunk = ff // 2 if (ff // 2) % LANES == 0 else ff
    row = lambda a: pl.BlockSpec((tm, a.shape[1]), lambda i: (i, 0))
    return pl.pallas_call(
        functools.partial(_ffn_dense_kernel, fchunk=fchunk),
        grid=(m // tm,),
        in_specs=[row(x), row(oa), row(ob), row(oc), _const_spec(wo.shape), _const_spec((1, d)),
                  _const_spec(wg.shape), _const_spec(wu.shape), _const_spec(wd.shape)],
        out_specs=row(x),
        out_shape=jax.ShapeDtypeStruct((m, d), F32),
        compiler_params=_params("parallel"),
        name="ffn_dense",
    )(x, oa, ob, oc, wo, g.reshape(1, d), wg, wu, wd)


def _route(hn, wr):
    logits = jnp.dot(hn, wr, preferred_element_type=F32, precision=lax.Precision.HIGHEST)
    lane = lax.broadcasted_iota(jnp.int32, logits.shape, 1).astype(F32)
    logits = jnp.where(lane < N_EXPERTS, logits, -jnp.inf)
    m1 = jnp.max(logits, axis=-1, keepdims=True)
    i1 = jnp.min(jnp.where(logits == m1, lane, float(LANES)), axis=-1, keepdims=True)
    rest = jnp.where(lane == i1, -jnp.inf, logits)
    m2 = jnp.max(rest, axis=-1, keepdims=True)
    i2 = jnp.min(jnp.where(rest == m2, lane, float(LANES)), axis=-1, keepdims=True)
    e2 = jnp.exp(m2 - m1)
    g1 = 1.0 / (1.0 + e2)
    g2 = e2 / (1.0 + e2)
    return jnp.where(lane == i1, g1, 0.0) + jnp.where(lane == i2, g2, 0.0)


def _ffn_moe_kernel(x_ref, oa_ref, ob_ref, oc_ref, wo_ref, g_ref, wr_ref, wg_ref, wu_ref, wd_ref,
                    y_ref, h_ref, hn_ref, comb_ref, acc_ref):
    e = pl.program_id(1)

    @pl.when(e == 0)
    def _():
        h = _mix_residual(x_ref, oa_ref, ob_ref, oc_ref, wo_ref)
        hn = _rms(h, g_ref[...])
        h_ref[...] = h
        hn_ref[...] = hn.astype(BF16)
        comb = _route(hn, wr_ref[...])
        for j in range(N_EXPERTS):
            comb_ref[j] = jnp.broadcast_to(comb[:, j:j + 1], comb.shape)
        acc_ref[...] = jnp.zeros_like(acc_ref)

    hn = hn_ref[...]
    gate = _dot(hn, wg_ref[...])
    up = _dot(hn, wu_ref[...])
    act = (gate * jax.nn.sigmoid(gate) * up).astype(BF16)
    out = _dot(act, wd_ref[...])
    cb = comb_ref[e]
    for j in range(out.shape[1] // LANES):
        lanes = slice(j * LANES, (j + 1) * LANES)
        acc_ref[:, lanes] += cb * out[:, lanes]

    @pl.when(e == pl.num_programs(1) - 1)
    def _():
        y_ref[...] = h_ref[...] + acc_ref[...]


def _ffn_moe(x, oa, ob, oc, wo, g, wr, wg, wu, wd, *, tm):
    m, d = x.shape
    tm = min(tm, m)
    ne, _, ff = wg.shape
    row = lambda a: pl.BlockSpec((tm, a.shape[1]), lambda i, e: (i, 0))
    const = lambda shape: pl.BlockSpec(shape, lambda i, e: (0,) * len(shape))
    return pl.pallas_call(
        _ffn_moe_kernel,
        grid=(m // tm, ne),
        in_specs=[row(x), row(oa), row(ob), row(oc), const(wo.shape), const((1, d)),
                  const(wr.shape),
                  pl.BlockSpec((None, d, ff), lambda i, e: (e, 0, 0)),
                  pl.BlockSpec((None, d, ff), lambda i, e: (e, 0, 0)),
                  pl.BlockSpec((None, ff, d), lambda i, e: (e, 0, 0))],
        out_specs=row(x),
        out_shape=jax.ShapeDtypeStruct((m, d), F32),
        scratch_shapes=[pltpu.VMEM((tm, d), F32), pltpu.VMEM((tm, d), BF16),
                        pltpu.VMEM((ne, tm, LANES), F32), pltpu.VMEM((tm, d), F32)],
        compiler_params=_params("parallel", "arbitrary"),
        name="ffn_moe",
    )(x, oa, ob, oc, wo, g.reshape(1, d), wr, wg, wu, wd)


def _tile_gain(g, width):
    return jnp.tile(g.astype(F32), width // g.shape[0]).reshape(1, width)


def kernel(x_prompt, x_sample, cache_k, cache_v, state_b, w_in, w_out, norm_mix, norm_ffn,
           q_norm, k_norm, hgrn_lb, hgrn_out_norm, sgu_norm, sgu_w, sgu_b,
           ffn_w_gate, ffn_w_up, ffn_w_down, moe_router, moe_w_gate, moe_w_up, moe_w_down):
    depth = w_in.shape[0]
    bsz, seq, d = x_prompt.shape
    dbsz, dseq, _ = x_sample.shape
    past = cache_k.shape[2]
    a_heads = cache_k.shape[3]
    b_heads = state_b.shape[2]
    aw = a_heads * HEAD_DIM
    bw = b_heads * HEAD_DIM
    c_groups = sgu_w.shape[1]
    cw = c_groups * HEAD_DIM
    c_chunk = sgu_w.shape[2]

    cs = jnp.cumsum(jax.nn.softmax(hgrn_lb.astype(F32), axis=0), axis=0)
    lb_all = cs - cs[0:1]

    ck_t = jnp.transpose(cache_k, (0, 1, 3, 4, 2)).reshape(depth, dbsz, aw, past)
    cv_t = jnp.transpose(cache_v, (0, 1, 3, 4, 2)).reshape(depth, dbsz, aw, past)

    def layer(l, xp, xs, k_all, v_all):
        w_in_l = w_in[l].astype(BF16)
        w_out_l = w_out[l].astype(BF16)
        qg = _tile_gain(q_norm[l], aw)
        kg = _tile_gain(k_norm[l], aw)
        hg = _tile_gain(hgrn_out_norm[l], bw)
        sg = _tile_gain(sgu_norm[l], cw)
        lb = lb_all[l]
        lbp = jnp.stack([jnp.log(lb), jnp.log1p(-lb), 1.0 - lb])

        def sgu_params(lc):
            tril = jnp.tril(jnp.ones((lc, lc), sgu_w.dtype))
            w = (sgu_w[l][:, :lc, :lc] * tril).astype(BF16)
            bias = jnp.repeat(sgu_b[l][:, :lc].T, HEAD_DIM, axis=1)
            return w, bias

        def ffn(x, oa, ob, oc):
            i = l // 2
            if l % 2 == 0:
                return _ffn_dense(x, oa, ob, oc, w_out_l, norm_ffn[l], ffn_w_gate[i].astype(BF16),
                                  ffn_w_up[i].astype(BF16), ffn_w_down[i].astype(BF16), tm=512)
            wr = jnp.zeros((d, LANES), F32).at[:, :N_EXPERTS].set(moe_router[i])
            return _ffn_moe(x, oa, ob, oc, w_out_l, norm_ffn[l], wr, moe_w_gate[i].astype(BF16),
                            moe_w_up[i].astype(BF16), moe_w_down[i].astype(BF16), tm=512)

        x = xp.reshape(bsz * seq, d)
        q, k_all, v_all, kb, vb, bp, cp = _inproj(
            x, norm_mix[l], w_in_l, qg, kg, aw=aw, bw=bw, cw=cw, tm=512,
            key_major=(l, depth, bsz, seq, k_all, v_all))
        oa = _attn_prompt(q, kb, vb, bsz=bsz, seq=seq)
        st0 = jnp.zeros((bsz, bw // LANES, LANES, LANES), F32)
        ob, st = _hgrn(bp, lbp, hg, st0, bsz=bsz, t=seq, tc=512)
        sw, sb = sgu_params(c_chunk)
        oc, _ = _sgu(cp, sg, sw, sb, lc=c_chunk, tm=8 * c_chunk)
        yp = ffn(x, oa, ob, oc).reshape(bsz, seq, d)
        outs_p = (_tiles_to_state(st),)

        x = xs.reshape(dbsz * dseq, d)
        q, k, v, kb, vb, bp, cp = _inproj(x, norm_mix[l], w_in_l, qg, kg, aw=aw, bw=bw, cw=cw, tm=512)
        kb = jnp.swapaxes(kb.reshape(dbsz, dseq, aw), 1, 2)
        vb = jnp.swapaxes(vb.reshape(dbsz, dseq, aw), 1, 2)
        oa = _attn_sample(q, kb, vb, ck_t, cv_t, layer=l, bsz=dbsz, t=dseq, past=past)
        ob, st = _hgrn(bp, lbp, hg, _state_to_tiles(state_b[l].astype(F32)), bsz=dbsz, t=dseq, tc=dseq)
        sw, sb = sgu_params(dseq)
        oc, vn = _sgu(cp, sg, sw, sb, lc=dseq, tm=dbsz * dseq)
        ys = ffn(x, oa, ob, oc).reshape(dbsz, dseq, d)
        outs_s = (k.reshape(dbsz, dseq, a_heads, HEAD_DIM), v.reshape(dbsz, dseq, a_heads, HEAD_DIM),
                  _tiles_to_state(st), vn.reshape(dbsz, dseq, cw))
        return yp, ys, k_all, v_all, outs_p, outs_s

    yp, ys = x_prompt, x_sample
    k_all = jnp.zeros((depth, bsz, aw, seq), F32)
    v_all = jnp.zeros((depth, bsz, aw, seq), F32)
    ps, ss = [], []
    for l in range(depth):
        yp, ys, k_all, v_all, op, os_ = layer(l, yp, ys, k_all, v_all)
        ps.append(op)
        ss.append(os_)
    stack = lambda items, j: jnp.stack([it[j] for it in items])
    rows_major = lambda a: jnp.transpose(a.reshape(depth, bsz, a_heads, HEAD_DIM, seq), (0, 1, 4, 2, 3))
    return (yp, ys, rows_major(k_all), rows_major(v_all), stack(ps, 0),
            stack(ss, 0), stack(ss, 1), stack(ss, 2), stack(ss, 3))
```

```python
import functools

import jax
import jax.numpy as jnp
from jax import lax
from jax.experimental import pallas as pl
from jax.experimental.pallas import tpu as pltpu

F32 = jnp.float32
BF16 = jnp.bfloat16

HEAD_DIM = 64
LANES = 128
EPS = 1e-6
N_EXPERTS = 8
TOP_K = 2
VMEM_LIMIT = 56 * 1024 * 1024

ATT_BLOCK = 128
ATT_DEAD = -120.0
HG_STEP = 16
NEG_BIG = -1e30


def _dot(a, b):
    return jnp.dot(a, b, preferred_element_type=F32)


def _dot_nt(a, b):
    return lax.dot_general(a, b, (((1,), (1,)), ((), ())), preferred_element_type=F32)


def _split2(x):
    hi = x.astype(BF16)
    lo = (x - hi.astype(F32)).astype(BF16)
    return hi, lo


def _split3(x):
    hi = x.astype(BF16)
    r = x - hi.astype(F32)
    mid = r.astype(BF16)
    lo = (r - mid.astype(F32)).astype(BF16)
    return hi, mid, lo


def _dot_x2(x, m):
    hi, lo = _split2(x)
    return _dot(hi, m) + _dot(lo, m)


def _ldot_x3(m, x):
    hi, mid, lo = _split3(x)
    return _dot(m, hi) + _dot(m, mid) + _dot(m, lo)


def _softplus_neg_abs(z):
    return jnp.log(1.0 + jnp.exp(-jnp.abs(z)))


def _group_rmsnorm(a, ind, gain_row):
    ms = _dot_x2(a * a, ind) * (1.0 / HEAD_DIM)
    return a * lax.rsqrt(ms + EPS) * gain_row


def _group_indicator(width):
    g = jnp.arange(width, dtype=jnp.int32) // HEAD_DIM
    return (g[:, None] == g[None, :]).astype(BF16)


def _const_spec(shape):
    nd = len(shape)
    return pl.BlockSpec(shape, lambda *_: (0,) * nd)


def _params(*sem):
    return pltpu.CompilerParams(dimension_semantics=sem, vmem_limit_bytes=VMEM_LIMIT)


def _inproj_kernel(x_ref, g_ref, w_ref, qg_ref, kg_ref, ind_ref, *rest, aw, bw, key_major):
    q_ref, k_ref, v_ref, kb_ref, vb_ref, b_ref, c_ref = rest[-7:]
    x = x_ref[...]
    ms = jnp.mean(x * x, axis=-1, keepdims=True)
    xn = (x * lax.rsqrt(ms + EPS) * g_ref[...]).astype(BF16)
    ind = ind_ref[...]
    aq = _dot(xn, w_ref[:, 0:aw])
    q = _group_rmsnorm(aq, ind, qg_ref[...])
    q_ref[...] = (q * (HEAD_DIM ** -0.5)).astype(BF16)
    ak = _dot(xn, w_ref[:, aw:2 * aw])
    k = _group_rmsnorm(ak, ind, kg_ref[...])
    v = _dot(xn, w_ref[:, 2 * aw:3 * aw])
    if key_major:
        k = k.T
        v = v.T
    if key_major == "first":
        k_ref[0] = k
        v_ref[0] = v
        for s in range(1, k_ref.shape[0]):
            k_ref[s] = jnp.zeros_like(k)
            v_ref[s] = jnp.zeros_like(v)
    else:
        k_ref[...] = k
        v_ref[...] = v
    kb_ref[...] = k.astype(BF16)
    vb_ref[...] = v.astype(BF16)
    b_ref[...] = _dot(xn, w_ref[:, 3 * aw:3 * aw + 4 * bw])
    c_ref[...] = _dot(xn, w_ref[:, 3 * aw + 4 * bw:])


def _inproj(x, g, w, qg, kg, *, aw, bw, cw, tm, key_major=None):
    m, d = x.shape
    tm = min(tm, m)
    nw = w.shape[1]
    row = lambda width: pl.BlockSpec((tm, width), lambda i: (i, 0))
    in_specs = [row(d), _const_spec((1, d)), _const_spec((d, nw)),
                _const_spec((1, aw)), _const_spec((1, aw)), _const_spec((aw, aw))]
    args = [x, g.reshape(1, d), w, qg, kg, _group_indicator(aw)]
    aliases = {}
    if key_major is None:
        kv_specs = [row(aw)] * 4
        kv_shapes = [jax.ShapeDtypeStruct((m, aw), F32)] * 2 + [jax.ShapeDtypeStruct((m, aw), BF16)] * 2
    else:
        layer, depth, bsz, seq, k_all, v_all = key_major
        nt = seq // tm
        if k_all is None:
            assert layer == 0
            mode = "first"
            slab = pl.BlockSpec((depth, None, aw, tm), lambda i: (0, i // nt, 0, i % nt))
        else:
            mode = "later"
            slab = pl.BlockSpec((None, None, aw, tm), lambda i: (layer, i // nt, 0, i % nt))
            in_specs += [pl.BlockSpec(memory_space=pl.ANY)] * 2
            aliases = {len(args): 1, len(args) + 1: 2}
            args += [k_all, v_all]
        kv_specs = [slab] * 2 + [pl.BlockSpec((None, aw, tm), lambda i: (i // nt, 0, i % nt))] * 2
        kv_shapes = ([jax.ShapeDtypeStruct((depth, bsz, aw, seq), F32)] * 2
                     + [jax.ShapeDtypeStruct((bsz, aw, seq), BF16)] * 2)
    return pl.pallas_call(
        functools.partial(_inproj_kernel, aw=aw, bw=bw, key_major=None if key_major is None else mode),
        grid=(m // tm,),
        in_specs=in_specs,
        out_specs=[row(aw)] + kv_specs + [row(4 * bw), row(2 * cw)],
        out_shape=([jax.ShapeDtypeStruct((m, aw), BF16)] + kv_shapes
                   + [jax.ShapeDtypeStruct((m, 4 * bw), F32), jax.ShapeDtypeStruct((m, 2 * cw), F32)]),
        input_output_aliases=aliases,
        compiler_params=_params("parallel"),
        name="inproj",
    )(*args)


def _head_masks(rows):
    lane = lax.broadcasted_iota(jnp.int32, (rows, LANES), 1)
    return lane < HEAD_DIM


def _sb_block(qs, kblk, vblk, carry_ref, acc_ref, cum2, low, mask):
    tk = kblk.shape[1]
    rows = qs[0].shape[0] // 2
    n = 2 * rows
    npair = len(qs)
    pair = lambda p: slice(p * LANES, (p + 1) * LANES)
    zs = [_dot(q, kblk[pair(p), :]) for p, q in enumerate(qs)]
    log_betas, rs = [], []
    for p in range(npair):
        z = zs[p]
        sp = jnp.log(1.0 + jnp.exp(-jnp.abs(z)))
        log_beta = jnp.minimum(z, 0.0) - sp
        log_keep = log_beta - z
        if mask is not None:
            log_keep = jnp.where(mask, log_keep, 0.0)
        hi, lo = _split2(log_keep)
        r = _dot(jnp.concatenate([hi, lo], axis=0), cum2)
        log_betas.append(log_beta)
        rs.append(r[:n] + r[n:])
    for p in range(npair):
        carry = carry_ref[p * n:(p + 1) * n, :]
        after = rs[p][:, LANES:] + (carry if tk == LANES else carry[:, :tk])
        w = jnp.exp(log_betas[p] + after)
        if mask is not None:
            w = jnp.where(mask, w, 0.0)
        pv = _dot_nt(w.astype(BF16), vblk[pair(p), :])
        acc_ref[:, pair(p)] += jnp.where(low, pv[:rows], pv[rows:])
        carry_ref[p * n:(p + 1) * n, :] = carry + rs[p][:, :LANES]


def _stack_heads(q, low):
    qs = []
    for p in range(q.shape[1] // LANES):
        qp = q[:, p * LANES:(p + 1) * LANES]
        zero = jnp.zeros_like(qp)
        qs.append(jnp.concatenate([jnp.where(low, qp, zero), jnp.where(low, zero, qp)], axis=0))
    return qs


def _causal_mask(rows, tk, heads):
    r = lax.broadcasted_iota(jnp.int32, (heads * rows, tk), 0) & (rows - 1)
    c = lax.broadcasted_iota(jnp.int32, (heads * rows, tk), 1)
    return c < r


def _attn_prompt_kernel(q_ref, k_ref, v_ref, cum_ref, o_ref, carry_ref, acc_ref, *, blk, heads):
    i = pl.program_id(1)
    low = _head_masks(blk)
    qs = _stack_heads(q_ref[...], low)
    carry_ref[...] = jnp.zeros_like(carry_ref)
    acc_ref[...] = jnp.zeros_like(acc_ref)
    cum2 = cum_ref[...]
    start = pl.multiple_of(i * blk, blk)
    _sb_block(qs, k_ref[:, pl.ds(start, blk)], v_ref[:, pl.ds(start, blk)],
              carry_ref, acc_ref, cum2, low, _causal_mask(blk, blk, 2))

    def older_block(kb):
        s = pl.multiple_of(kb * blk, blk)
        _sb_block(qs, k_ref[:, pl.ds(s, blk)], v_ref[:, pl.ds(s, blk)],
                  carry_ref, acc_ref, cum2, low, None)

    def cond(state):
        kb, live = state
        return jnp.logical_and(kb >= 1, live > ATT_DEAD)

    def body(state):
        kb, _ = state
        older_block(kb)
        older_block(kb - 1)
        return kb - 2, jnp.max(carry_ref[...])

    kb_end, live_end = lax.while_loop(cond, body, (i - 1, jnp.max(carry_ref[...])))

    @pl.when(jnp.logical_and(kb_end == 0, live_end > ATT_DEAD))
    def _():
        older_block(0)

    o_ref[...] = acc_ref[...].astype(o_ref.dtype)


def _cum_matrix(n):
    j = jnp.arange(n, dtype=jnp.int32)
    newer = (j[:, None] > j[None, :]).astype(BF16)
    return jnp.concatenate([jnp.ones((n, LANES), BF16), newer], axis=1)


def _attn_prompt(q, kb, vb, *, bsz, seq):
    m, aw = q.shape
    blk = ATT_BLOCK
    nq = seq // blk
    nh = aw // HEAD_DIM
    return pl.pallas_call(
        functools.partial(_attn_prompt_kernel, blk=blk, heads=nh),
        grid=(bsz, nq),
        in_specs=[pl.BlockSpec((blk, aw), lambda b, i: (b * nq + i, 0)),
                  pl.BlockSpec((None, aw, seq), lambda b, i: (b, 0, 0)),
                  pl.BlockSpec((None, aw, seq), lambda b, i: (b, 0, 0)),
                  _const_spec((blk, LANES + blk))],
        out_specs=pl.BlockSpec((blk, aw), lambda b, i: (b * nq + i, 0)),
        out_shape=jax.ShapeDtypeStruct((m, aw), BF16),
        scratch_shapes=[pltpu.VMEM((nh * blk, LANES), F32), pltpu.VMEM((blk, aw), F32)],
        compiler_params=_params("parallel", "arbitrary"),
        name="attn_prompt",
    )(q, kb, vb, _cum_matrix(blk))


def _attn_sample_kernel(q_ref, kn_ref, vn_ref, ck_hbm, cv_hbm, cum_new_ref, cum_ref,
                        o_ref, carry_ref, acc_ref, kbuf, vbuf, sem, *, layer, t, blk, past, heads):
    b = pl.program_id(0)
    nblk = past // blk

    def slot_of(kb):
        return lax.rem(nblk - 1 - kb, 2)

    def copies(kb):
        slot = slot_of(kb)
        cols = pl.ds(pl.multiple_of(kb * blk, blk), blk)
        return (pltpu.make_async_copy(ck_hbm.at[layer, b, :, cols], kbuf.at[slot], sem.at[0, slot]),
                pltpu.make_async_copy(cv_hbm.at[layer, b, :, cols], vbuf.at[slot], sem.at[1, slot]))

    def start(kb):
        for c in copies(kb):
            c.start()

    def wait(kb):
        for c in copies(kb):
            c.wait()

    start(nblk - 1)
    low = _head_masks(t)
    qs = _stack_heads(q_ref[...], low)
    carry_ref[...] = jnp.zeros_like(carry_ref)
    acc_ref[...] = jnp.zeros_like(acc_ref)
    _sb_block(qs, kn_ref[...], vn_ref[...], carry_ref, acc_ref,
              cum_new_ref[...], low, _causal_mask(t, t, 2))
    cum2 = cum_ref[...]

    def cond(state):
        kb, live = state
        return jnp.logical_and(kb >= 0, live > ATT_DEAD)

    def body(state):
        kb, _ = state
        wait(kb)

        @pl.when(kb > 0)
        def _():
            start(kb - 1)

        slot = slot_of(kb)
        _sb_block(qs, kbuf[slot].astype(BF16), vbuf[slot].astype(BF16),
                  carry_ref, acc_ref, cum2, low, None)
        return kb - 1, jnp.max(carry_ref[...])

    kb_end, _ = lax.while_loop(cond, body, (nblk - 1, jnp.max(carry_ref[...])))

    @pl.when(kb_end >= 0)
    def _():
        wait(kb_end)

    o_ref[...] = acc_ref[...].astype(o_ref.dtype)


def _attn_sample(q, kb, vb, ck, cv, *, layer, bsz, t, past):
    m, aw = q.shape
    blk = ATT_BLOCK
    nh = aw // HEAD_DIM
    assert past >= blk and past % blk == 0
    new = pl.BlockSpec((t, aw), lambda b: (b, 0))
    new_t = pl.BlockSpec((None, aw, t), lambda b: (b, 0, 0))
    hbm = pl.BlockSpec(memory_space=pl.ANY)
    return pl.pallas_call(
        functools.partial(_attn_sample_kernel, layer=layer, t=t, blk=blk, past=past, heads=nh),
        grid=(bsz,),
        in_specs=[new, new_t, new_t, hbm, hbm,
                  _const_spec((t, LANES + t)), _const_spec((blk, LANES + blk))],
        out_specs=new,
        out_shape=jax.ShapeDtypeStruct((m, aw), BF16),
        scratch_shapes=[pltpu.VMEM((nh * t, LANES), F32), pltpu.VMEM((t, aw), F32),
                        pltpu.VMEM((2, aw, blk), F32), pltpu.VMEM((2, aw, blk), F32),
                        pltpu.SemaphoreType.DMA((2, 2))],
        compiler_params=_params("arbitrary"),
        name="attn_sample",
    )(q, kb, vb, ck, cv, _cum_matrix(t), _cum_matrix(blk))


def _hgrn_group(blk, lbp, gain, tril, ind, bd_mask, st_ref):
    bw = gain.shape[1]
    n = HG_STEP
    half = n // 2
    steps = blk.shape[0] // n
    npair = bw // LANES
    pair = lambda p: slice(p * LANES, (p + 1) * LANES)
    qr = blk[:, 0:bw]
    z = blk[:, bw:2 * bw]
    v = blk[:, 2 * bw:3 * bw]
    gr = blk[:, 3 * bw:4 * bw]
    log_lb, log1m_lb, one_m_lb = lbp[0:1], lbp[1:2], lbp[2:3]

    sp = _softplus_neg_abs(z)
    cc = log1m_lb + (jnp.minimum(z, 0.0) - sp)
    log_f = jnp.maximum(log_lb, cc) + _softplus_neg_abs(log_lb - cc)
    k = one_m_lb * jnp.exp(jnp.minimum(-z, 0.0) - sp)
    q = qr * jax.nn.sigmoid(qr)
    b = _ldot_x3(tril, log_f)
    vb = v.astype(BF16)

    row = lax.broadcasted_iota(jnp.int32, (n, bw), 0)
    row_hi = lax.broadcasted_iota(jnp.int32, (half, bw), 0) + half
    parts, qbs, decays, upds = [], [], [], []
    for j in range(steps):
        sl = slice(j * n, (j + 1) * n)
        bj, qj, kj = b[sl], q[sl], k[sl]
        b_last = bj[n - 1:n]
        qbs.append((qj * jnp.exp(bj)).astype(BF16))
        kd = (kj * jnp.exp(b_last - bj)).astype(BF16)
        decays.append(jnp.exp(b_last))
        upds.append([lax.dot_general(vb[sl, pair(p)], kd[:, pair(p)], (((0,), (0,)), ((), ())),
                                     preferred_element_type=F32) for p in range(npair)])
        for s in range(half):
            e = jnp.exp(jnp.where(row >= s, bj - bj[s:s + 1], NEG_BIG))
            parts.append((qj * e) * kj[s:s + 1])
        bh, qh = bj[half:], qj[half:]
        for s in range(half, n):
            e = jnp.exp(jnp.where(row_hi >= s, bh - bj[s:s + 1], NEG_BIG))
            parts.append((qh * e) * kj[s:s + 1])
    att = _dot(jnp.concatenate(parts, axis=0).astype(BF16), ind)
    per_step = half * n + half * half

    sts = [st_ref[p] for p in range(npair)]
    outs = []
    for j in range(steps):
        base = j * per_step
        vj = v[j * n:(j + 1) * n]
        o_lo = att[base:base + n] * vj[0:1]
        for s in range(1, half):
            o_lo = o_lo + att[base + s * n:base + (s + 1) * n] * vj[s:s + 1]
        base += half * n
        o_hi = att[base:base + half] * vj[half:half + 1]
        for s in range(half + 1, n):
            o_hi = o_hi + att[base + (s - half) * half:base + (s - half + 1) * half] * vj[s:s + 1]
        o_intra = jnp.concatenate([o_lo[:half], o_lo[half:] + o_hi], axis=0)
        o_inter = []
        for p in range(npair):
            o_inter.append(_dot_nt(qbs[j][:, pair(p)], sts[p].astype(BF16)))
            sts[p] = sts[p] * decays[j][:, pair(p)] + jnp.where(bd_mask, upds[j][p], 0.0)
        outs.append(o_intra + jnp.concatenate(o_inter, axis=1))
    for p in range(npair):
        st_ref[p] = sts[p]
    o = jnp.concatenate(outs, axis=0) if steps > 1 else outs[0]

    ms = _dot_x2(o * o, ind) * (1.0 / HEAD_DIM)
    o = o * lax.rsqrt(ms + EPS) * gain
    return o * (gr * jax.nn.sigmoid(gr))


def _hgrn_kernel(b_ref, lbp_ref, gain_ref, tril_ref, ind_ref, st0_ref, o_ref, st_out_ref, st_ref,
                 *, groups, grows):
    i = pl.program_id(1)

    @pl.when(i == 0)
    def _():
        st_ref[...] = st0_ref[...]

    lbp = lbp_ref[...]
    gain = gain_ref[...]
    tril = tril_ref[...]
    ind = ind_ref[...]
    r = lax.broadcasted_iota(jnp.int32, (LANES, LANES), 0) // HEAD_DIM
    c = lax.broadcasted_iota(jnp.int32, (LANES, LANES), 1) // HEAD_DIM
    bd_mask = r == c

    def body(j, _):
        rows = pl.ds(pl.multiple_of(j * grows, grows), grows)
        o = _hgrn_group(b_ref[rows, :], lbp, gain, tril, ind, bd_mask, st_ref)
        o_ref[rows, :] = o.astype(o_ref.dtype)
        return 0

    lax.fori_loop(0, groups, body, 0)

    @pl.when(i == pl.num_programs(1) - 1)
    def _():
        st_out_ref[...] = st_ref[...]


def _hgrn(bproj, lbp, gain, st0, *, bsz, t, tc, gsteps=16):
    m, w4 = bproj.shape
    bw = w4 // 4
    tc = min(tc, t)
    nc = t // tc
    grows = min(gsteps * HG_STEP, tc)
    npair = bw // LANES
    j = jnp.arange(grows, dtype=jnp.int32)
    tril = jnp.logical_and(j[:, None] >= j[None, :],
                           j[:, None] // HG_STEP == j[None, :] // HG_STEP).astype(BF16)
    st_spec = pl.BlockSpec((None, npair, LANES, LANES), lambda b, i: (b, 0, 0, 0))
    return pl.pallas_call(
        functools.partial(_hgrn_kernel, groups=tc // grows, grows=grows),
        grid=(bsz, nc),
        in_specs=[pl.BlockSpec((tc, w4), lambda b, i: (b * nc + i, 0)),
                  _const_spec((3, bw)), _const_spec((1, bw)),
                  _const_spec((grows, grows)), _const_spec((bw, bw)), st_spec],
        out_specs=[pl.BlockSpec((tc, bw), lambda b, i: (b * nc + i, 0)), st_spec],
        out_shape=[jax.ShapeDtypeStruct((m, bw), BF16),
                   jax.ShapeDtypeStruct((bsz, npair, LANES, LANES), F32)],
        scratch_shapes=[pltpu.VMEM((npair, LANES, LANES), F32)],
        compiler_params=_params("parallel", "arbitrary"),
        name="hgrn2",
    )(bproj, lbp, gain, tril, _group_indicator(bw), st0)


def _state_to_tiles(s):
    bsz, h, dk, dv = s.shape
    st = jnp.swapaxes(s, 2, 3).reshape(bsz, h // 2, 2, dv, dk)
    z = jnp.zeros_like(st[:, :, 0])
    top = jnp.concatenate([st[:, :, 0], z], axis=-1)
    bot = jnp.concatenate([z, st[:, :, 1]], axis=-1)
    return jnp.concatenate([top, bot], axis=-2)


def _tiles_to_state(tiles):
    a = tiles[:, :, :HEAD_DIM, :HEAD_DIM]
    b = tiles[:, :, HEAD_DIM:, HEAD_DIM:]
    st = jnp.stack([a, b], axis=2)
    bsz, npair = tiles.shape[:2]
    return jnp.swapaxes(st.reshape(bsz, 2 * npair, HEAD_DIM, HEAD_DIM), 2, 3)


def _sgu_kernel(c_ref, gain_ref, w_ref, bias_ref, ind_ref, y_ref, vn_ref, *, cw, lc):
    c = c_ref[...]
    u = jax.nn.gelu(c[:, :cw])
    gv = jax.nn.gelu(c[:, cw:])
    vn = _group_rmsnorm(gv, ind_ref[...], gain_ref[...])
    vn_ref[...] = vn
    low = _head_masks(lc)
    bias = bias_ref[...]
    for j in range(c.shape[0] // lc):
        rows = slice(j * lc, (j + 1) * lc)
        mixed = []
        for p in range(cw // LANES):
            vp = vn[rows, p * LANES:(p + 1) * LANES].astype(BF16)
            mixed.append(jnp.where(low, _dot(w_ref[2 * p], vp), _dot(w_ref[2 * p + 1], vp)))
        y_ref[rows, :] = (u[rows] * (jnp.concatenate(mixed, axis=1) + bias)).astype(y_ref.dtype)


def _sgu(cproj, gain, w, bias, *, lc, tm):
    m, w2 = cproj.shape
    cw = w2 // 2
    ng = cw // HEAD_DIM
    tm = min(tm, m)
    row = lambda width: pl.BlockSpec((tm, width), lambda i: (i, 0))
    return pl.pallas_call(
        functools.partial(_sgu_kernel, cw=cw, lc=lc),
        grid=(m // tm,),
        in_specs=[row(w2), _const_spec((1, cw)), _const_spec((ng, lc, lc)),
                  _const_spec((lc, cw)), _const_spec((cw, cw))],
        out_specs=[row(cw), row(cw)],
        out_shape=[jax.ShapeDtypeStruct((m, cw), BF16), jax.ShapeDtypeStruct((m, cw), F32)],
        compiler_params=_params("parallel"),
        name="sgu",
    )(cproj, gain, w, bias, _group_indicator(cw))


def _mix_residual(x_ref, oa_ref, ob_ref, oc_ref, wo_ref):
    aw = oa_ref.shape[1]
    bw = ob_ref.shape[1]
    return (x_ref[...] + _dot(oa_ref[...], wo_ref[0:aw, :])
            + _dot(ob_ref[...], wo_ref[aw:aw + bw, :])
            + _dot(oc_ref[...], wo_ref[aw + bw:, :]))


def _rms(h, g):
    return h * lax.rsqrt(jnp.mean(h * h, axis=-1, keepdims=True) + EPS) * g


def _ffn_dense_kernel(x_ref, oa_ref, ob_ref, oc_ref, wo_ref, g_ref, wg_ref, wu_ref, wd_ref, y_ref,
                      *, fchunk):
    h = _mix_residual(x_ref, oa_ref, ob_ref, oc_ref, wo_ref)
    hn = _rms(h, g_ref[...]).astype(BF16)
    y_ref[...] = h
    for f0 in range(0, wg_ref.shape[1], fchunk):
        cols = slice(f0, f0 + fchunk)
        gate = _dot(hn, wg_ref[:, cols])
        up = _dot(hn, wu_ref[:, cols])
        act = (gate * jax.nn.sigmoid(gate) * up).astype(BF16)
        y_ref[...] += _dot(act, wd_ref[cols, :])


def _ffn_dense(x, oa, ob, oc, wo, g, wg, wu, wd, *, tm):
    m, d = x.shape
    tm = min(tm, m)
    ff = wg.shape[1]
    fchunk = ff // 2 if (ff // 2) % LANES == 0 else ff
    row = lambda a: pl.BlockSpec((tm, a.shape[1]), lambda i: (i, 0))
    return pl.pallas_call(
        functools.partial(_ffn_dense_kernel, fchunk=fchunk),
        grid=(m // tm,),
        in_specs=[row(x), row(oa), row(ob), row(oc), _const_spec(wo.shape), _const_spec((1, d)),
                  _const_spec(wg.shape), _const_spec(wu.shape), _const_spec(wd.shape)],
        out_specs=row(x),
        out_shape=jax.ShapeDtypeStruct((m, d), F32),
        compiler_params=_params("parallel"),
        name="ffn_dense",
    )(x, oa, ob, oc, wo, g.reshape(1, d), wg, wu, wd)


def _route(hn, wr):
    logits = jnp.dot(hn, wr, preferred_element_type=F32, precision=lax.Precision.HIGHEST)
    lane = lax.broadcasted_iota(jnp.int32, logits.shape, 1).astype(F32)
    logits = jnp.where(lane < N_EXPERTS, logits, -jnp.inf)
    m1 = jnp.max(logits, axis=-1, keepdims=True)
    i1 = jnp.min(jnp.where(logits == m1, lane, float(LANES)), axis=-1, keepdims=True)
    rest = jnp.where(lane == i1, -jnp.inf, logits)
    m2 = jnp.max(rest, axis=-1, keepdims=True)
    i2 = jnp.min(jnp.where(rest == m2, lane, float(LANES)), axis=-1, keepdims=True)
    e2 = jnp.exp(m2 - m1)
    g1 = 1.0 / (1.0 + e2)
    g2 = e2 / (1.0 + e2)
    return jnp.where(lane == i1, g1, 0.0) + jnp.where(lane == i2, g2, 0.0)


def _ffn_moe_kernel(x_ref, oa_ref, ob_ref, oc_ref, wo_ref, g_ref, wr_ref, wg_ref, wu_ref, wd_ref,
                    y_ref, h_ref, hn_ref, comb_ref, acc_ref):
    e = pl.program_id(1)

    @pl.when(e == 0)
    def _():
        h = _mix_residual(x_ref, oa_ref, ob_ref, oc_ref, wo_ref)
        hn = _rms(h, g_ref[...])
        h_ref[...] = h
        hn_ref[...] = hn.astype(BF16)
        comb = _route(hn, wr_ref[...])
        for j in range(N_EXPERTS):
            comb_ref[j] = jnp.broadcast_to(comb[:, j:j + 1], comb.shape)
        acc_ref[...] = jnp.zeros_like(acc_ref)

    hn = hn_ref[...]
    gate = _dot(hn, wg_ref[...])
    up = _dot(hn, wu_ref[...])
    act = (gate * jax.nn.sigmoid(gate) * up).astype(BF16)
    out = _dot(act, wd_ref[...])
    cb = comb_ref[e]
    for j in range(out.shape[1] // LANES):
        lanes = slice(j * LANES, (j + 1) * LANES)
        acc_ref[:, lanes] += cb * out[:, lanes]

    @pl.when(e == pl.num_programs(1) - 1)
    def _():
        y_ref[...] = h_ref[...] + acc_ref[...]


def _ffn_moe(x, oa, ob, oc, wo, g, wr, wg, wu, wd, *, tm):
    m, d = x.shape
    tm = min(tm, m)
    ne, _, ff = wg.shape
    row = lambda a: pl.BlockSpec((tm, a.shape[1]), lambda i, e: (i, 0))
    const = lambda shape: pl.BlockSpec(shape, lambda i, e: (0,) * len(shape))
    return pl.pallas_call(
        _ffn_moe_kernel,
        grid=(m // tm, ne),
        in_specs=[row(x), row(oa), row(ob), row(oc), const(wo.shape), const((1, d)),
                  const(wr.shape),
                  pl.BlockSpec((None, d, ff), lambda i, e: (e, 0, 0)),
                  pl.BlockSpec((None, d, ff), lambda i, e: (e, 0, 0)),
                  pl.BlockSpec((None, ff, d), lambda i, e: (e, 0, 0))],
        out_specs=row(x),
        out_shape=jax.ShapeDtypeStruct((m, d), F32),
        scratch_shapes=[pltpu.VMEM((tm, d), F32), pltpu.VMEM((tm, d), BF16),
                        pltpu.VMEM((ne, tm, LANES), F32), pltpu.VMEM((tm, d), F32)],
        compiler_params=_params("parallel", "arbitrary"),
        name="ffn_moe",
    )(x, oa, ob, oc, wo, g.reshape(1, d), wr, wg, wu, wd)


def _tile_gain(g, width):
    return jnp.tile(g.astype(F32), width // g.shape[0]).reshape(1, width)


def kernel(x_prompt, x_sample, cache_k, cache_v, state_b, w_in, w_out, norm_mix, norm_ffn,
           q_norm, k_norm, hgrn_lb, hgrn_out_norm, sgu_norm, sgu_w, sgu_b,
           ffn_w_gate, ffn_w_up, ffn_w_down, moe_router, moe_w_gate, moe_w_up, moe_w_down):
    depth = w_in.shape[0]
    bsz, seq, d = x_prompt.shape
    dbsz, dseq, _ = x_sample.shape
    past = cache_k.shape[2]
    a_heads = cache_k.shape[3]
    b_heads = state_b.shape[2]
    aw = a_heads * HEAD_DIM
    bw = b_heads * HEAD_DIM
    c_groups = sgu_w.shape[1]
    cw = c_groups * HEAD_DIM
    c_chunk = sgu_w.shape[2]

    cs = jnp.cumsum(jax.nn.softmax(hgrn_lb.astype(F32), axis=0), axis=0)
    lb_all = cs - cs[0:1]

    ck_t = jnp.transpose(cache_k, (0, 1, 3, 4, 2)).reshape(depth, dbsz, aw, past)
    cv_t = jnp.transpose(cache_v, (0, 1, 3, 4, 2)).reshape(depth, dbsz, aw, past)

    def layer(l, xp, xs, k_all, v_all):
        w_in_l = w_in[l].astype(BF16)
        w_out_l = w_out[l].astype(BF16)
        qg = _tile_gain(q_norm[l], aw)
        kg = _tile_gain(k_norm[l], aw)
        hg = _tile_gain(hgrn_out_norm[l], bw)
        sg = _tile_gain(sgu_norm[l], cw)
        lb = lb_all[l]
        lbp = jnp.stack([jnp.log(lb), jnp.log1p(-lb), 1.0 - lb])

        def sgu_params(lc):
            tril = jnp.tril(jnp.ones((lc, lc), sgu_w.dtype))
            w = (sgu_w[l][:, :lc, :lc] * tril).astype(BF16)
            bias = jnp.repeat(sgu_b[l][:, :lc].T, HEAD_DIM, axis=1)
            return w, bias

        def ffn(x, oa, ob, oc):
            i = l // 2
            if l % 2 == 0:
                return _ffn_dense(x, oa, ob, oc, w_out_l, norm_ffn[l], ffn_w_gate[i].astype(BF16),
                                  ffn_w_up[i].astype(BF16), ffn_w_down[i].astype(BF16), tm=512)
            wr = jnp.zeros((d, LANES), F32).at[:, :N_EXPERTS].set(moe_router[i])
            return _ffn_moe(x, oa, ob, oc, w_out_l, norm_ffn[l], wr, moe_w_gate[i].astype(BF16),
                            moe_w_up[i].astype(BF16), moe_w_down[i].astype(BF16), tm=512)

        x = xp.reshape(bsz * seq, d)
        q, k_all, v_all, kb, vb, bp, cp = _inproj(
            x, norm_mix[l], w_in_l, qg, kg, aw=aw, bw=bw, cw=cw, tm=512,
            key_major=(l, depth, bsz, seq, k_all, v_all))
        oa = _attn_prompt(q, kb, vb, bsz=bsz, seq=seq)
        st0 = jnp.zeros((bsz, bw // LANES, LANES, LANES), F32)
        ob, st = _hgrn(bp, lbp, hg, st0, bsz=bsz, t=seq, tc=512)
        sw, sb = sgu_params(c_chunk)
        oc, _ = _sgu(cp, sg, sw, sb, lc=c_chunk, tm=8 * c_chunk)
        yp = ffn(x, oa, ob, oc).reshape(bsz, seq, d)
        outs_p = (_tiles_to_state(st),)

        x = xs.reshape(dbsz * dseq, d)
        q, k, v, kb, vb, bp, cp = _inproj(x, norm_mix[l], w_in_l, qg, kg, aw=aw, bw=bw, cw=cw, tm=512)
        kb = jnp.swapaxes(kb.reshape(dbsz, dseq, aw), 1, 2)
        vb = jnp.swapaxes(vb.reshape(dbsz, dseq, aw), 1, 2)
        oa = _attn_sample(q, kb, vb, ck_t, cv_t, layer=l, bsz=dbsz, t=dseq, past=past)
        ob, st = _hgrn(bp, lbp, hg, _state_to_tiles(state_b[l].astype(F32)), bsz=dbsz, t=dseq, tc=dseq)
        sw, sb = sgu_params(dseq)
        oc, vn = _sgu(cp, sg, sw, sb, lc=dseq, tm=dbsz * dseq)
        ys = ffn(x, oa, ob, oc).reshape(dbsz, dseq, d)
        outs_s = (k.reshape(dbsz, dseq, a_heads, HEAD_DIM), v.reshape(dbsz, dseq, a_heads, HEAD_DIM),
                  _tiles_to_state(st), vn.reshape(dbsz, dseq, cw))
        return yp, ys, k_all, v_all, outs_p, outs_s

    yp, ys = x_prompt, x_sample
    k_all = v_all = None
    ps, ss = [], []
    for l in range(depth):
        yp, ys, k_all, v_all, op, os_ = layer(l, yp, ys, k_all, v_all)
        ps.append(op)
        ss.append(os_)
    stack = lambda items, j: jnp.stack([it[j] for it in items])
    rows_major = lambda a: jnp.transpose(a.reshape(depth, bsz, a_heads, HEAD_DIM, seq), (0, 1, 4, 2, 3))
    return (yp, ys, rows_major(k_all), rows_major(v_all), stack(ps, 0),
            stack(ss, 0), stack(ss, 1), stack(ss, 2), stack(ss, 3))
```

```python
import functools

import jax
import jax.numpy as jnp
from jax import lax
from jax.experimental import pallas as pl
from jax.experimental.pallas import tpu as pltpu

F32 = jnp.float32
BF16 = jnp.bfloat16

HEAD_DIM = 64
LANES = 128
EPS = 1e-6
N_EXPERTS = 8
TOP_K = 2
VMEM_LIMIT = 56 * 1024 * 1024

ATT_BLOCK = 128
ATT_DEAD = -120.0
HG_STEP = 16
MOE_FCHUNK = 256
NEG_BIG = -1e30


def _dot(a, b):
    return jnp.dot(a, b, preferred_element_type=F32)


def _dot_nt(a, b):
    return lax.dot_general(a, b, (((1,), (1,)), ((), ())), preferred_element_type=F32)


def _split2(x):
    hi = x.astype(BF16)
    lo = (x - hi.astype(F32)).astype(BF16)
    return hi, lo


def _split3(x):
    hi = x.astype(BF16)
    r = x - hi.astype(F32)
    mid = r.astype(BF16)
    lo = (r - mid.astype(F32)).astype(BF16)
    return hi, mid, lo


def _dot_x2(x, m):
    hi, lo = _split2(x)
    return _dot(hi, m) + _dot(lo, m)


def _ldot_x3(m, x):
    hi, mid, lo = _split3(x)
    return _dot(m, hi) + _dot(m, mid) + _dot(m, lo)


def _softplus_neg_abs(z):
    return jnp.log(1.0 + jnp.exp(-jnp.abs(z)))


def _group_rmsnorm(a, ind, gain_row):
    ms = _dot_x2(a * a, ind) * (1.0 / HEAD_DIM)
    return a * lax.rsqrt(ms + EPS) * gain_row


def _group_indicator(width):
    g = jnp.arange(width, dtype=jnp.int32) // HEAD_DIM
    return (g[:, None] == g[None, :]).astype(BF16)


def _const_spec(shape):
    nd = len(shape)
    return pl.BlockSpec(shape, lambda *_: (0,) * nd)


def _params(*sem):
    return pltpu.CompilerParams(dimension_semantics=sem, vmem_limit_bytes=VMEM_LIMIT)


def _inproj_kernel(x_ref, g_ref, w_ref, qg_ref, kg_ref, ind_ref, *rest, aw, bw, key_major):
    q_ref, k_ref, v_ref, kb_ref, vb_ref, b_ref, c_ref = rest[-7:]
    x = x_ref[...]
    ms = jnp.mean(x * x, axis=-1, keepdims=True)
    xn = (x * lax.rsqrt(ms + EPS) * g_ref[...]).astype(BF16)
    ind = ind_ref[...]
    aq = _dot(xn, w_ref[:, 0:aw])
    q = _group_rmsnorm(aq, ind, qg_ref[...])
    q_ref[...] = (q * (HEAD_DIM ** -0.5)).astype(BF16)
    ak = _dot(xn, w_ref[:, aw:2 * aw])
    k = _group_rmsnorm(ak, ind, kg_ref[...])
    v = _dot(xn, w_ref[:, 2 * aw:3 * aw])
    if key_major:
        k = k.T
        v = v.T
    if key_major == "first":
        k_ref[0] = k
        v_ref[0] = v
        for s in range(1, k_ref.shape[0]):
            k_ref[s] = jnp.zeros_like(k)
            v_ref[s] = jnp.zeros_like(v)
    else:
        k_ref[...] = k
        v_ref[...] = v
    kb_ref[...] = k.astype(BF16)
    vb_ref[...] = v.astype(BF16)
    b_ref[...] = _dot(xn, w_ref[:, 3 * aw:3 * aw + 4 * bw])
    c_ref[...] = _dot(xn, w_ref[:, 3 * aw + 4 * bw:])


def _inproj(x, g, w, qg, kg, *, aw, bw, cw, tm, key_major=None):
    m, d = x.shape
    tm = min(tm, m)
    nw = w.shape[1]
    row = lambda width: pl.BlockSpec((tm, width), lambda i: (i, 0))
    in_specs = [row(d), _const_spec((1, d)), _const_spec((d, nw)),
                _const_spec((1, aw)), _const_spec((1, aw)), _const_spec((aw, aw))]
    args = [x, g.reshape(1, d), w, qg, kg, _group_indicator(aw)]
    aliases = {}
    if key_major is None:
        kv_specs = [row(aw)] * 4
        kv_shapes = [jax.ShapeDtypeStruct((m, aw), F32)] * 2 + [jax.ShapeDtypeStruct((m, aw), BF16)] * 2
    else:
        layer, depth, bsz, seq, k_all, v_all = key_major
        nt = seq // tm
        if k_all is None:
            assert layer == 0
            mode = "first"
            slab = pl.BlockSpec((depth, None, aw, tm), lambda i: (0, i // nt, 0, i % nt))
        else:
            mode = "later"
            slab = pl.BlockSpec((None, None, aw, tm), lambda i: (layer, i // nt, 0, i % nt))
            in_specs += [pl.BlockSpec(memory_space=pl.ANY)] * 2
            aliases = {len(args): 1, len(args) + 1: 2}
            args += [k_all, v_all]
        kv_specs = [slab] * 2 + [pl.BlockSpec((None, aw, tm), lambda i: (i // nt, 0, i % nt))] * 2
        kv_shapes = ([jax.ShapeDtypeStruct((depth, bsz, aw, seq), F32)] * 2
                     + [jax.ShapeDtypeStruct((bsz, aw, seq), BF16)] * 2)
    return pl.pallas_call(
        functools.partial(_inproj_kernel, aw=aw, bw=bw, key_major=None if key_major is None else mode),
        grid=(m // tm,),
        in_specs=in_specs,
        out_specs=[row(aw)] + kv_specs + [row(4 * bw), row(2 * cw)],
        out_shape=([jax.ShapeDtypeStruct((m, aw), BF16)] + kv_shapes
                   + [jax.ShapeDtypeStruct((m, 4 * bw), F32), jax.ShapeDtypeStruct((m, 2 * cw), F32)]),
        input_output_aliases=aliases,
        compiler_params=_params("parallel"),
        name="inproj",
    )(*args)


def _head_masks(rows):
    lane = lax.broadcasted_iota(jnp.int32, (rows, LANES), 1)
    return lane < HEAD_DIM


def _sb_block(qs, kblk, vblk, carry_ref, acc_ref, cum2, low, mask):
    tk = kblk.shape[1]
    rows = qs[0].shape[0] // 2
    n = 2 * rows
    npair = len(qs)
    pair = lambda p: slice(p * LANES, (p + 1) * LANES)
    zs = [_dot(q, kblk[pair(p), :]) for p, q in enumerate(qs)]
    log_betas, rs = [], []
    for p in range(npair):
        z = zs[p]
        sp = jnp.log(1.0 + jnp.exp(-jnp.abs(z)))
        log_beta = jnp.minimum(z, 0.0) - sp
        log_keep = log_beta - z
        if mask is not None:
            log_keep = jnp.where(mask, log_keep, 0.0)
        hi, lo = _split2(log_keep)
        r = _dot(jnp.concatenate([hi, lo], axis=0), cum2)
        log_betas.append(log_beta)
        rs.append(r[:n] + r[n:])
    for p in range(npair):
        carry = carry_ref[p * n:(p + 1) * n, :]
        after = rs[p][:, LANES:] + (carry if tk == LANES else carry[:, :tk])
        w = jnp.exp(log_betas[p] + after)
        if mask is not None:
            w = jnp.where(mask, w, 0.0)
        pv = _dot_nt(w.astype(BF16), vblk[pair(p), :])
        acc_ref[:, pair(p)] += jnp.where(low, pv[:rows], pv[rows:])
        carry_ref[p * n:(p + 1) * n, :] = carry + rs[p][:, :LANES]


def _stack_heads(q, low):
    qs = []
    for p in range(q.shape[1] // LANES):
        qp = q[:, p * LANES:(p + 1) * LANES]
        zero = jnp.zeros_like(qp)
        qs.append(jnp.concatenate([jnp.where(low, qp, zero), jnp.where(low, zero, qp)], axis=0))
    return qs


def _causal_mask(rows, tk, heads):
    r = lax.broadcasted_iota(jnp.int32, (heads * rows, tk), 0) & (rows - 1)
    c = lax.broadcasted_iota(jnp.int32, (heads * rows, tk), 1)
    return c < r


def _attn_prompt_kernel(q_ref, k_ref, v_ref, cum_ref, o_ref, carry_ref, acc_ref, *, blk, heads):
    i = pl.program_id(1)
    low = _head_masks(blk)
    qs = _stack_heads(q_ref[...], low)
    carry_ref[...] = jnp.zeros_like(carry_ref)
    acc_ref[...] = jnp.zeros_like(acc_ref)
    cum2 = cum_ref[...]
    start = pl.multiple_of(i * blk, blk)
    _sb_block(qs, k_ref[:, pl.ds(start, blk)], v_ref[:, pl.ds(start, blk)],
              carry_ref, acc_ref, cum2, low, _causal_mask(blk, blk, 2))

    def older_block(kb):
        s = pl.multiple_of(kb * blk, blk)
        _sb_block(qs, k_ref[:, pl.ds(s, blk)], v_ref[:, pl.ds(s, blk)],
                  carry_ref, acc_ref, cum2, low, None)

    def cond(state):
        kb, live = state
        return jnp.logical_and(kb >= 1, live > ATT_DEAD)

    def body(state):
        kb, _ = state
        older_block(kb)
        older_block(kb - 1)
        return kb - 2, jnp.max(carry_ref[...])

    kb_end, live_end = lax.while_loop(cond, body, (i - 1, jnp.max(carry_ref[...])))

    @pl.when(jnp.logical_and(kb_end == 0, live_end > ATT_DEAD))
    def _():
        older_block(0)

    o_ref[...] = acc_ref[...].astype(o_ref.dtype)


def _cum_matrix(n):
    j = jnp.arange(n, dtype=jnp.int32)
    newer = (j[:, None] > j[None, :]).astype(BF16)
    return jnp.concatenate([jnp.ones((n, LANES), BF16), newer], axis=1)


def _attn_prompt(q, kb, vb, *, bsz, seq):
    m, aw = q.shape
    blk = ATT_BLOCK
    nq = seq // blk
    nh = aw // HEAD_DIM
    return pl.pallas_call(
        functools.partial(_attn_prompt_kernel, blk=blk, heads=nh),
        grid=(bsz, nq),
        in_specs=[pl.BlockSpec((blk, aw), lambda b, i: (b * nq + i, 0)),
                  pl.BlockSpec((None, aw, seq), lambda b, i: (b, 0, 0)),
                  pl.BlockSpec((None, aw, seq), lambda b, i: (b, 0, 0)),
                  _const_spec((blk, LANES + blk))],
        out_specs=pl.BlockSpec((blk, aw), lambda b, i: (b * nq + i, 0)),
        out_shape=jax.ShapeDtypeStruct((m, aw), BF16),
        scratch_shapes=[pltpu.VMEM((nh * blk, LANES), F32), pltpu.VMEM((blk, aw), F32)],
        compiler_params=_params("parallel", "arbitrary"),
        name="attn_prompt",
    )(q, kb, vb, _cum_matrix(blk))


def _attn_sample_kernel(q_ref, kn_ref, vn_ref, ck_hbm, cv_hbm, cum_new_ref, cum_ref,
                        o_ref, carry_ref, acc_ref, kbuf, vbuf, sem, *, layer, t, blk, past, heads):
    b = pl.program_id(0)
    nblk = past // blk

    def slot_of(kb):
        return lax.rem(nblk - 1 - kb, 2)

    def copies(kb):
        slot = slot_of(kb)
        cols = pl.ds(pl.multiple_of(kb * blk, blk), blk)
        return (pltpu.make_async_copy(ck_hbm.at[layer, b, :, cols], kbuf.at[slot], sem.at[0, slot]),
                pltpu.make_async_copy(cv_hbm.at[layer, b, :, cols], vbuf.at[slot], sem.at[1, slot]))

    def start(kb):
        for c in copies(kb):
            c.start()

    def wait(kb):
        for c in copies(kb):
            c.wait()

    start(nblk - 1)
    low = _head_masks(t)
    qs = _stack_heads(q_ref[...], low)
    carry_ref[...] = jnp.zeros_like(carry_ref)
    acc_ref[...] = jnp.zeros_like(acc_ref)
    _sb_block(qs, kn_ref[...], vn_ref[...], carry_ref, acc_ref,
              cum_new_ref[...], low, _causal_mask(t, t, 2))
    cum2 = cum_ref[...]

    def cond(state):
        kb, live = state
        return jnp.logical_and(kb >= 0, live > ATT_DEAD)

    def body(state):
        kb, _ = state
        wait(kb)

        @pl.when(kb > 0)
        def _():
            start(kb - 1)

        slot = slot_of(kb)
        _sb_block(qs, kbuf[slot].astype(BF16), vbuf[slot].astype(BF16),
                  carry_ref, acc_ref, cum2, low, None)
        return kb - 1, jnp.max(carry_ref[...])

    kb_end, _ = lax.while_loop(cond, body, (nblk - 1, jnp.max(carry_ref[...])))

    @pl.when(kb_end >= 0)
    def _():
        wait(kb_end)

    o_ref[...] = acc_ref[...].astype(o_ref.dtype)


def _attn_sample(q, kb, vb, ck, cv, *, layer, bsz, t, past):
    m, aw = q.shape
    blk = ATT_BLOCK
    nh = aw // HEAD_DIM
    assert past >= blk and past % blk == 0
    new = pl.BlockSpec((t, aw), lambda b: (b, 0))
    new_t = pl.BlockSpec((None, aw, t), lambda b: (b, 0, 0))
    hbm = pl.BlockSpec(memory_space=pl.ANY)
    return pl.pallas_call(
        functools.partial(_attn_sample_kernel, layer=layer, t=t, blk=blk, past=past, heads=nh),
        grid=(bsz,),
        in_specs=[new, new_t, new_t, hbm, hbm,
                  _const_spec((t, LANES + t)), _const_spec((blk, LANES + blk))],
        out_specs=new,
        out_shape=jax.ShapeDtypeStruct((m, aw), BF16),
        scratch_shapes=[pltpu.VMEM((nh * t, LANES), F32), pltpu.VMEM((t, aw), F32),
                        pltpu.VMEM((2, aw, blk), F32), pltpu.VMEM((2, aw, blk), F32),
                        pltpu.SemaphoreType.DMA((2, 2))],
        compiler_params=_params("arbitrary"),
        name="attn_sample",
    )(q, kb, vb, ck, cv, _cum_matrix(t), _cum_matrix(blk))


def _hgrn_group(blk, lbp, gain, tril, ind, bd_mask, st_ref):
    bw = gain.shape[1]
    n = HG_STEP
    half = n // 2
    steps = blk.shape[0] // n
    npair = bw // LANES
    pair = lambda p: slice(p * LANES, (p + 1) * LANES)
    qr = blk[:, 0:bw]
    z = blk[:, bw:2 * bw]
    v = blk[:, 2 * bw:3 * bw]
    gr = blk[:, 3 * bw:4 * bw]
    log_lb, log1m_lb, one_m_lb = lbp[0:1], lbp[1:2], lbp[2:3]

    sp = _softplus_neg_abs(z)
    cc = log1m_lb + (jnp.minimum(z, 0.0) - sp)
    log_f = jnp.maximum(log_lb, cc) + _softplus_neg_abs(log_lb - cc)
    k = one_m_lb * jnp.exp(jnp.minimum(-z, 0.0) - sp)
    q = qr * jax.nn.sigmoid(qr)
    b = _ldot_x3(tril, log_f)
    vb = v.astype(BF16)

    row = lax.broadcasted_iota(jnp.int32, (n, bw), 0)
    row_hi = lax.broadcasted_iota(jnp.int32, (half, bw), 0) + half
    parts, qbs, decays, upds = [], [], [], []
    for j in range(steps):
        sl = slice(j * n, (j + 1) * n)
        bj, qj, kj = b[sl], q[sl], k[sl]
        b_last = bj[n - 1:n]
        qbs.append((qj * jnp.exp(bj)).astype(BF16))
        kd = (kj * jnp.exp(b_last - bj)).astype(BF16)
        decays.append(jnp.exp(b_last))
        upds.append([lax.dot_general(vb[sl, pair(p)], kd[:, pair(p)], (((0,), (0,)), ((), ())),
                                     preferred_element_type=F32) for p in range(npair)])
        for s in range(half):
            e = jnp.exp(jnp.where(row >= s, bj - bj[s:s + 1], NEG_BIG))
            parts.append((qj * e) * kj[s:s + 1])
        bh, qh = bj[half:], qj[half:]
        for s in range(half, n):
            e = jnp.exp(jnp.where(row_hi >= s, bh - bj[s:s + 1], NEG_BIG))
            parts.append((qh * e) * kj[s:s + 1])
    att = _dot(jnp.concatenate(parts, axis=0).astype(BF16), ind)
    per_step = half * n + half * half

    sts = [st_ref[p] for p in range(npair)]
    outs = []
    for j in range(steps):
        base = j * per_step
        vj = v[j * n:(j + 1) * n]
        o_lo = att[base:base + n] * vj[0:1]
        for s in range(1, half):
            o_lo = o_lo + att[base + s * n:base + (s + 1) * n] * vj[s:s + 1]
        base += half * n
        o_hi = att[base:base + half] * vj[half:half + 1]
        for s in range(half + 1, n):
            o_hi = o_hi + att[base + (s - half) * half:base + (s - half + 1) * half] * vj[s:s + 1]
        o_intra = jnp.concatenate([o_lo[:half], o_lo[half:] + o_hi], axis=0)
        o_inter = []
        for p in range(npair):
            o_inter.append(_dot_nt(qbs[j][:, pair(p)], sts[p].astype(BF16)))
            sts[p] = sts[p] * decays[j][:, pair(p)] + jnp.where(bd_mask, upds[j][p], 0.0)
        outs.append(o_intra + jnp.concatenate(o_inter, axis=1))
    for p in range(npair):
        st_ref[p] = sts[p]
    o = jnp.concatenate(outs, axis=0) if steps > 1 else outs[0]

    ms = _dot_x2(o * o, ind) * (1.0 / HEAD_DIM)
    o = o * lax.rsqrt(ms + EPS) * gain
    return o * (gr * jax.nn.sigmoid(gr))


def _hgrn_kernel(b_ref, lbp_ref, gain_ref, tril_ref, ind_ref, st0_ref, o_ref, st_out_ref, st_ref,
                 *, groups, grows):
    i = pl.program_id(1)

    @pl.when(i == 0)
    def _():
        st_ref[...] = st0_ref[...]

    lbp = lbp_ref[...]
    gain = gain_ref[...]
    tril = tril_ref[...]
    ind = ind_ref[...]
    r = lax.broadcasted_iota(jnp.int32, (LANES, LANES), 0) // HEAD_DIM
    c = lax.broadcasted_iota(jnp.int32, (LANES, LANES), 1) // HEAD_DIM
    bd_mask = r == c

    def body(j, _):
        rows = pl.ds(pl.multiple_of(j * grows, grows), grows)
        o = _hgrn_group(b_ref[rows, :], lbp, gain, tril, ind, bd_mask, st_ref)
        o_ref[rows, :] = o.astype(o_ref.dtype)
        return 0

    lax.fori_loop(0, groups, body, 0)

    @pl.when(i == pl.num_programs(1) - 1)
    def _():
        st_out_ref[...] = st_ref[...]


def _hgrn(bproj, lbp, gain, st0, *, bsz, t, tc, gsteps=16):
    m, w4 = bproj.shape
    bw = w4 // 4
    tc = min(tc, t)
    nc = t // tc
    grows = min(gsteps * HG_STEP, tc)
    npair = bw // LANES
    j = jnp.arange(grows, dtype=jnp.int32)
    tril = jnp.logical_and(j[:, None] >= j[None, :],
                           j[:, None] // HG_STEP == j[None, :] // HG_STEP).astype(BF16)
    st_spec = pl.BlockSpec((None, npair, LANES, LANES), lambda b, i: (b, 0, 0, 0))
    return pl.pallas_call(
        functools.partial(_hgrn_kernel, groups=tc // grows, grows=grows),
        grid=(bsz, nc),
        in_specs=[pl.BlockSpec((tc, w4), lambda b, i: (b * nc + i, 0)),
                  _const_spec((3, bw)), _const_spec((1, bw)),
                  _const_spec((grows, grows)), _const_spec((bw, bw)), st_spec],
        out_specs=[pl.BlockSpec((tc, bw), lambda b, i: (b * nc + i, 0)), st_spec],
        out_shape=[jax.ShapeDtypeStruct((m, bw), BF16),
                   jax.ShapeDtypeStruct((bsz, npair, LANES, LANES), F32)],
        scratch_shapes=[pltpu.VMEM((npair, LANES, LANES), F32)],
        compiler_params=_params("parallel", "arbitrary"),
        name="hgrn2",
    )(bproj, lbp, gain, tril, _group_indicator(bw), st0)


def _state_to_tiles(s):
    bsz, h, dk, dv = s.shape
    st = jnp.swapaxes(s, 2, 3).reshape(bsz, h // 2, 2, dv, dk)
    z = jnp.zeros_like(st[:, :, 0])
    top = jnp.concatenate([st[:, :, 0], z], axis=-1)
    bot = jnp.concatenate([z, st[:, :, 1]], axis=-1)
    return jnp.concatenate([top, bot], axis=-2)


def _tiles_to_state(tiles):
    a = tiles[:, :, :HEAD_DIM, :HEAD_DIM]
    b = tiles[:, :, HEAD_DIM:, HEAD_DIM:]
    st = jnp.stack([a, b], axis=2)
    bsz, npair = tiles.shape[:2]
    return jnp.swapaxes(st.reshape(bsz, 2 * npair, HEAD_DIM, HEAD_DIM), 2, 3)


def _sgu_kernel(c_ref, gain_ref, w_ref, bias_ref, ind_ref, y_ref, vn_ref, *, cw, lc):
    c = c_ref[...]
    u = jax.nn.gelu(c[:, :cw])
    gv = jax.nn.gelu(c[:, cw:])
    vn = _group_rmsnorm(gv, ind_ref[...], gain_ref[...])
    vn_ref[...] = vn
    low = _head_masks(lc)
    bias = bias_ref[...]
    for j in range(c.shape[0] // lc):
        rows = slice(j * lc, (j + 1) * lc)
        mixed = []
        for p in range(cw // LANES):
            vp = vn[rows, p * LANES:(p + 1) * LANES].astype(BF16)
            mixed.append(jnp.where(low, _dot(w_ref[2 * p], vp), _dot(w_ref[2 * p + 1], vp)))
        y_ref[rows, :] = (u[rows] * (jnp.concatenate(mixed, axis=1) + bias)).astype(y_ref.dtype)


def _sgu(cproj, gain, w, bias, *, lc, tm):
    m, w2 = cproj.shape
    cw = w2 // 2
    ng = cw // HEAD_DIM
    tm = min(tm, m)
    row = lambda width: pl.BlockSpec((tm, width), lambda i: (i, 0))
    return pl.pallas_call(
        functools.partial(_sgu_kernel, cw=cw, lc=lc),
        grid=(m // tm,),
        in_specs=[row(w2), _const_spec((1, cw)), _const_spec((ng, lc, lc)),
                  _const_spec((lc, cw)), _const_spec((cw, cw))],
        out_specs=[row(cw), row(cw)],
        out_shape=[jax.ShapeDtypeStruct((m, cw), BF16), jax.ShapeDtypeStruct((m, cw), F32)],
        compiler_params=_params("parallel"),
        name="sgu",
    )(cproj, gain, w, bias, _group_indicator(cw))


def _mix_residual(x_ref, oa_ref, ob_ref, oc_ref, wo_ref):
    aw = oa_ref.shape[1]
    bw = ob_ref.shape[1]
    return (x_ref[...] + _dot(oa_ref[...], wo_ref[0:aw, :])
            + _dot(ob_ref[...], wo_ref[aw:aw + bw, :])
            + _dot(oc_ref[...], wo_ref[aw + bw:, :]))


def _rms(h, g):
    return h * lax.rsqrt(jnp.mean(h * h, axis=-1, keepdims=True) + EPS) * g


def _ffn_dense_kernel(x_ref, oa_ref, ob_ref, oc_ref, wo_ref, g_ref, wg_ref, wu_ref, wd_ref, y_ref,
                      *, fchunk):
    h = _mix_residual(x_ref, oa_ref, ob_ref, oc_ref, wo_ref)
    hn = _rms(h, g_ref[...]).astype(BF16)
    y_ref[...] = h
    for f0 in range(0, wg_ref.shape[1], fchunk):
        cols = slice(f0, f0 + fchunk)
        gate = _dot(hn, wg_ref[:, cols])
        up = _dot(hn, wu_ref[:, cols])
        act = (gate * jax.nn.sigmoid(gate) * up).astype(BF16)
        y_ref[...] += _dot(act, wd_ref[cols, :])


def _ffn_dense(x, oa, ob, oc, wo, g, wg, wu, wd, *, tm):
    m, d = x.shape
    tm = min(tm, m)
    ff = wg.shape[1]
    fchunk = ff // 2 if (ff // 2) % LANES == 0 else ff
    row = lambda a: pl.BlockSpec((tm, a.shape[1]), lambda i: (i, 0))
    return pl.pallas_call(
        functools.partial(_ffn_dense_kernel, fchunk=fchunk),
        grid=(m // tm,),
        in_specs=[row(x), row(oa), row(ob), row(oc), _const_spec(wo.shape), _const_spec((1, d)),
                  _const_spec(wg.shape), _const_spec(wu.shape), _const_spec(wd.shape)],
        out_specs=row(x),
        out_shape=jax.ShapeDtypeStruct((m, d), F32),
        compiler_params=_params("parallel"),
        name="ffn_dense",
    )(x, oa, ob, oc, wo, g.reshape(1, d), wg, wu, wd)


def _route(hn, wr):
    logits = jnp.dot(hn, wr, preferred_element_type=F32, precision=lax.Precision.HIGHEST)
    lane = lax.broadcasted_iota(jnp.int32, logits.shape, 1).astype(F32)
    logits = jnp.where(lane < N_EXPERTS, logits, -jnp.inf)
    m1 = jnp.max(logits, axis=-1, keepdims=True)
    i1 = jnp.min(jnp.where(logits == m1, lane, float(LANES)), axis=-1, keepdims=True)
    rest = jnp.where(lane == i1, -jnp.inf, logits)
    m2 = jnp.max(rest, axis=-1, keepdims=True)
    i2 = jnp.min(jnp.where(rest == m2, lane, float(LANES)), axis=-1, keepdims=True)
    e2 = jnp.exp(m2 - m1)
    g1 = 1.0 / (1.0 + e2)
    g2 = e2 / (1.0 + e2)
    return jnp.where(lane == i1, g1, 0.0) + jnp.where(lane == i2, g2, 0.0)


def _ffn_moe_kernel(x_ref, oa_ref, ob_ref, oc_ref, wo_ref, g_ref, wr_ref, wg_ref, wu_ref, wd_ref,
                    y_ref, h_ref, hn_ref, comb_ref, acc_ref):
    e = pl.program_id(1)

    @pl.when(e == 0)
    def _():
        h = _mix_residual(x_ref, oa_ref, ob_ref, oc_ref, wo_ref)
        hn = _rms(h, g_ref[...])
        h_ref[...] = h
        hn_ref[...] = hn.astype(BF16)
        comb = _route(hn, wr_ref[...])
        for j in range(N_EXPERTS):
            comb_ref[j] = jnp.broadcast_to(comb[:, j:j + 1], comb.shape)
        acc_ref[...] = jnp.zeros_like(acc_ref)

    hn = hn_ref[...]
    parts = []
    for f0 in range(0, wg_ref.shape[1], MOE_FCHUNK):
        cols = slice(f0, f0 + MOE_FCHUNK)
        gate = _dot(hn, wg_ref[:, cols])
        up = _dot(hn, wu_ref[:, cols])
        act = (gate * jax.nn.sigmoid(gate) * up).astype(BF16)
        parts.append(_dot(act, wd_ref[cols, :]))
    while len(parts) > 1:
        parts = [a + b for a, b in zip(parts[0::2], parts[1::2])] + parts[len(parts) // 2 * 2:]
    out = parts[0]
    cb = comb_ref[e]
    for j in range(out.shape[1] // LANES):
        lanes = slice(j * LANES, (j + 1) * LANES)
        acc_ref[:, lanes] += cb * out[:, lanes]

    @pl.when(e == pl.num_programs(1) - 1)
    def _():
        y_ref[...] = h_ref[...] + acc_ref[...]


def _ffn_moe(x, oa, ob, oc, wo, g, wr, wg, wu, wd, *, tm):
    m, d = x.shape
    tm = min(tm, m)
    ne, _, ff = wg.shape
    row = lambda a: pl.BlockSpec((tm, a.shape[1]), lambda i, e: (i, 0))
    const = lambda shape: pl.BlockSpec(shape, lambda i, e: (0,) * len(shape))
    return pl.pallas_call(
        _ffn_moe_kernel,
        grid=(m // tm, ne),
        in_specs=[row(x), row(oa), row(ob), row(oc), const(wo.shape), const((1, d)),
                  const(wr.shape),
                  pl.BlockSpec((None, d, ff), lambda i, e: (e, 0, 0)),
                  pl.BlockSpec((None, d, ff), lambda i, e: (e, 0, 0)),
                  pl.BlockSpec((None, ff, d), lambda i, e: (e, 0, 0))],
        out_specs=row(x),
        out_shape=jax.ShapeDtypeStruct((m, d), F32),
        scratch_shapes=[pltpu.VMEM((tm, d), F32), pltpu.VMEM((tm, d), BF16),
                        pltpu.VMEM((ne, tm, LANES), F32), pltpu.VMEM((tm, d), F32)],
        compiler_params=_params("parallel", "arbitrary"),
        name="ffn_moe",
    )(x, oa, ob, oc, wo, g.reshape(1, d), wr, wg, wu, wd)


def _tile_gain(g, width):
    return jnp.tile(g.astype(F32), width // g.shape[0]).reshape(1, width)


def kernel(x_prompt, x_sample, cache_k, cache_v, state_b, w_in, w_out, norm_mix, norm_ffn,
           q_norm, k_norm, hgrn_lb, hgrn_out_norm, sgu_norm, sgu_w, sgu_b,
           ffn_w_gate, ffn_w_up, ffn_w_down, moe_router, moe_w_gate, moe_w_up, moe_w_down):
    depth = w_in.shape[0]
    bsz, seq, d = x_prompt.shape
    dbsz, dseq, _ = x_sample.shape
    past = cache_k.shape[2]
    a_heads = cache_k.shape[3]
    b_heads = state_b.shape[2]
    aw = a_heads * HEAD_DIM
    bw = b_heads * HEAD_DIM
    c_groups = sgu_w.shape[1]
    cw = c_groups * HEAD_DIM
    c_chunk = sgu_w.shape[2]

    cs = jnp.cumsum(jax.nn.softmax(hgrn_lb.astype(F32), axis=0), axis=0)
    lb_all = cs - cs[0:1]

    ck_t = jnp.transpose(cache_k, (0, 1, 3, 4, 2)).reshape(depth, dbsz, aw, past)
    cv_t = jnp.transpose(cache_v, (0, 1, 3, 4, 2)).reshape(depth, dbsz, aw, past)

    def layer(l, xp, xs, k_all, v_all):
        w_in_l = w_in[l].astype(BF16)
        w_out_l = w_out[l].astype(BF16)
        qg = _tile_gain(q_norm[l], aw)
        kg = _tile_gain(k_norm[l], aw)
        hg = _tile_gain(hgrn_out_norm[l], bw)
        sg = _tile_gain(sgu_norm[l], cw)
        lb = lb_all[l]
        lbp = jnp.stack([jnp.log(lb), jnp.log1p(-lb), 1.0 - lb])

        def sgu_params(lc):
            tril = jnp.tril(jnp.ones((lc, lc), sgu_w.dtype))
            w = (sgu_w[l][:, :lc, :lc] * tril).astype(BF16)
            bias = jnp.repeat(sgu_b[l][:, :lc].T, HEAD_DIM, axis=1)
            return w, bias

        def ffn(x, oa, ob, oc):
            i = l // 2
            if l % 2 == 0:
                return _ffn_dense(x, oa, ob, oc, w_out_l, norm_ffn[l], ffn_w_gate[i].astype(BF16),
                                  ffn_w_up[i].astype(BF16), ffn_w_down[i].astype(BF16), tm=512)
            wr = jnp.zeros((d, LANES), F32).at[:, :N_EXPERTS].set(moe_router[i])
            return _ffn_moe(x, oa, ob, oc, w_out_l, norm_ffn[l], wr, moe_w_gate[i].astype(BF16),
                            moe_w_up[i].astype(BF16), moe_w_down[i].astype(BF16), tm=512)

        x = xp.reshape(bsz * seq, d)
        q, k_all, v_all, kb, vb, bp, cp = _inproj(
            x, norm_mix[l], w_in_l, qg, kg, aw=aw, bw=bw, cw=cw, tm=512,
            key_major=(l, depth, bsz, seq, k_all, v_all))
        oa = _attn_prompt(q, kb, vb, bsz=bsz, seq=seq)
        st0 = jnp.zeros((bsz, bw // LANES, LANES, LANES), F32)
        ob, st = _hgrn(bp, lbp, hg, st0, bsz=bsz, t=seq, tc=512)
        sw, sb = sgu_params(c_chunk)
        oc, _ = _sgu(cp, sg, sw, sb, lc=c_chunk, tm=8 * c_chunk)
        yp = ffn(x, oa, ob, oc).reshape(bsz, seq, d)
        outs_p = (_tiles_to_state(st),)

        x = xs.reshape(dbsz * dseq, d)
        q, k, v, kb, vb, bp, cp = _inproj(x, norm_mix[l], w_in_l, qg, kg, aw=aw, bw=bw, cw=cw, tm=512)
        kb = jnp.swapaxes(kb.reshape(dbsz, dseq, aw), 1, 2)
        vb = jnp.swapaxes(vb.reshape(dbsz, dseq, aw), 1, 2)
        oa = _attn_sample(q, kb, vb, ck_t, cv_t, layer=l, bsz=dbsz, t=dseq, past=past)
        ob, st = _hgrn(bp, lbp, hg, _state_to_tiles(state_b[l].astype(F32)), bsz=dbsz, t=dseq, tc=dseq)
        sw, sb = sgu_params(dseq)
        oc, vn = _sgu(cp, sg, sw, sb, lc=dseq, tm=dbsz * dseq)
        ys = ffn(x, oa, ob, oc).reshape(dbsz, dseq, d)
        outs_s = (k.reshape(dbsz, dseq, a_heads, HEAD_DIM), v.reshape(dbsz, dseq, a_heads, HEAD_DIM),
                  _tiles_to_state(st), vn.reshape(dbsz, dseq, cw))
        return yp, ys, k_all, v_all, outs_p, outs_s

    yp, ys = x_prompt, x_sample
    k_all = v_all = None
    ps, ss = [], []
    for l in range(depth):
        yp, ys, k_all, v_all, op, os_ = layer(l, yp, ys, k_all, v_all)
        ps.append(op)
        ss.append(os_)
    stack = lambda items, j: jnp.stack([it[j] for it in items])
    rows_major = lambda a: jnp.transpose(a.reshape(depth, bsz, a_heads, HEAD_DIM, seq), (0, 1, 4, 2, 3))
    return (yp, ys, rows_major(k_all), rows_major(v_all), stack(ps, 0),
            stack(ss, 0), stack(ss, 1), stack(ss, 2), stack(ss, 3))
```

```python
import functools

import jax
import jax.numpy as jnp
from jax import lax
from jax.experimental import pallas as pl
from jax.experimental.pallas import tpu as pltpu

F32 = jnp.float32
BF16 = jnp.bfloat16

HEAD_DIM = 64
LANES = 128
EPS = 1e-6
N_EXPERTS = 8
TOP_K = 2
VMEM_LIMIT = 56 * 1024 * 1024

ATT_BLOCK = 128
ATT_DEAD = -120.0
HG_STEP = 16
MOE_FCHUNK = 256
NEG_BIG = -1e30


def _dot(a, b):
    return jnp.dot(a, b, preferred_element_type=F32)


def _dot_nt(a, b):
    return lax.dot_general(a, b, (((1,), (1,)), ((), ())), preferred_element_type=F32)


def _split2(x):
    hi = x.astype(BF16)
    lo = (x - hi.astype(F32)).astype(BF16)
    return hi, lo


def _split3(x):
    hi = x.astype(BF16)
    r = x - hi.astype(F32)
    mid = r.astype(BF16)
    lo = (r - mid.astype(F32)).astype(BF16)
    return hi, mid, lo


def _dot_x2(x, m):
    hi, lo = _split2(x)
    return _dot(hi, m) + _dot(lo, m)


def _ldot_x3(m, x):
    hi, mid, lo = _split3(x)
    return _dot(m, hi) + _dot(m, mid) + _dot(m, lo)


def _softplus_neg_abs(z):
    return jnp.log(1.0 + jnp.exp(-jnp.abs(z)))


def _group_rmsnorm(a, ind, gain_row):
    ms = _dot_x2(a * a, ind) * (1.0 / HEAD_DIM)
    return a * lax.rsqrt(ms + EPS) * gain_row


def _group_indicator(width):
    g = jnp.arange(width, dtype=jnp.int32) // HEAD_DIM
    return (g[:, None] == g[None, :]).astype(BF16)


def _const_spec(shape):
    nd = len(shape)
    return pl.BlockSpec(shape, lambda *_: (0,) * nd)


def _params(*sem):
    return pltpu.CompilerParams(dimension_semantics=sem, vmem_limit_bytes=VMEM_LIMIT)


def _inproj_kernel(x_ref, g_ref, w_ref, qg_ref, kg_ref, ind_ref, *rest, aw, bw, key_major):
    q_ref, k_ref, v_ref, kb_ref, vb_ref, b_ref, c_ref = rest[-7:]
    x = x_ref[...]
    ms = jnp.mean(x * x, axis=-1, keepdims=True)
    xn = (x * lax.rsqrt(ms + EPS) * g_ref[...]).astype(BF16)
    ind = ind_ref[...]
    aq = _dot(xn, w_ref[:, 0:aw])
    q = _group_rmsnorm(aq, ind, qg_ref[...])
    q_ref[...] = (q * (HEAD_DIM ** -0.5)).astype(BF16)
    ak = _dot(xn, w_ref[:, aw:2 * aw])
    k = _group_rmsnorm(ak, ind, kg_ref[...])
    v = _dot(xn, w_ref[:, 2 * aw:3 * aw])
    if key_major:
        k = k.T
        v = v.T
    if key_major == "first":
        k_ref[0] = k
        v_ref[0] = v
        for s in range(1, k_ref.shape[0]):
            k_ref[s] = jnp.zeros_like(k)
            v_ref[s] = jnp.zeros_like(v)
    else:
        k_ref[...] = k
        v_ref[...] = v
    kb_ref[...] = k.astype(BF16)
    vb_ref[...] = v.astype(BF16)
    b_ref[...] = _dot(xn, w_ref[:, 3 * aw:3 * aw + 4 * bw])
    c_ref[...] = _dot(xn, w_ref[:, 3 * aw + 4 * bw:])


def _inproj(x, g, w, qg, kg, *, aw, bw, cw, tm, key_major=None):
    m, d = x.shape
    tm = min(tm, m)
    nw = w.shape[1]
    row = lambda width: pl.BlockSpec((tm, width), lambda i: (i, 0))
    in_specs = [row(d), _const_spec((1, d)), _const_spec((d, nw)),
                _const_spec((1, aw)), _const_spec((1, aw)), _const_spec((aw, aw))]
    args = [x, g.reshape(1, d), w, qg, kg, _group_indicator(aw)]
    aliases = {}
    if key_major is None:
        kv_specs = [row(aw)] * 4
        kv_shapes = [jax.ShapeDtypeStruct((m, aw), F32)] * 2 + [jax.ShapeDtypeStruct((m, aw), BF16)] * 2
    else:
        layer, depth, bsz, seq, k_all, v_all = key_major
        nt = seq // tm
        if k_all is None:
            assert layer == 0
            mode = "first"
            slab = pl.BlockSpec((depth, None, aw, tm), lambda i: (0, i // nt, 0, i % nt))
        else:
            mode = "later"
            slab = pl.BlockSpec((None, None, aw, tm), lambda i: (layer, i // nt, 0, i % nt))
            in_specs += [pl.BlockSpec(memory_space=pl.ANY)] * 2
            aliases = {len(args): 1, len(args) + 1: 2}
            args += [k_all, v_all]
        kv_specs = [slab] * 2 + [pl.BlockSpec((None, aw, tm), lambda i: (i // nt, 0, i % nt))] * 2
        kv_shapes = ([jax.ShapeDtypeStruct((depth, bsz, aw, seq), F32)] * 2
                     + [jax.ShapeDtypeStruct((bsz, aw, seq), BF16)] * 2)
    return pl.pallas_call(
        functools.partial(_inproj_kernel, aw=aw, bw=bw, key_major=None if key_major is None else mode),
        grid=(m // tm,),
        in_specs=in_specs,
        out_specs=[row(aw)] + kv_specs + [row(4 * bw), row(2 * cw)],
        out_shape=([jax.ShapeDtypeStruct((m, aw), BF16)] + kv_shapes
                   + [jax.ShapeDtypeStruct((m, 4 * bw), F32), jax.ShapeDtypeStruct((m, 2 * cw), F32)]),
        input_output_aliases=aliases,
        compiler_params=_params("parallel"),
        name="inproj",
    )(*args)


def _head_masks(rows):
    lane = lax.broadcasted_iota(jnp.int32, (rows, LANES), 1)
    return lane < HEAD_DIM


def _sb_block(qs, kblk, vblk, carry_ref, acc_ref, cum2, low, mask):
    tk = kblk.shape[1]
    rows = qs[0].shape[0] // 2
    n = 2 * rows
    npair = len(qs)
    pair = lambda p: slice(p * LANES, (p + 1) * LANES)
    zs = [_dot(q, kblk[pair(p), :]) for p, q in enumerate(qs)]
    log_betas, rs = [], []
    for p in range(npair):
        z = zs[p]
        sp = jnp.log(1.0 + jnp.exp(-jnp.abs(z)))
        log_beta = jnp.minimum(z, 0.0) - sp
        log_keep = log_beta - z
        if mask is not None:
            log_keep = jnp.where(mask, log_keep, 0.0)
        hi, lo = _split2(log_keep)
        r = _dot(jnp.concatenate([hi, lo], axis=0), cum2)
        log_betas.append(log_beta)
        rs.append(r[:n] + r[n:])
    for p in range(npair):
        carry = carry_ref[p * n:(p + 1) * n, :]
        after = rs[p][:, LANES:] + (carry if tk == LANES else carry[:, :tk])
        w = jnp.exp(log_betas[p] + after)
        if mask is not None:
            w = jnp.where(mask, w, 0.0)
        pv = _dot_nt(w.astype(BF16), vblk[pair(p), :])
        acc_ref[:, pair(p)] += jnp.where(low, pv[:rows], pv[rows:])
        carry_ref[p * n:(p + 1) * n, :] = carry + rs[p][:, :LANES]


def _stack_heads(q, low):
    qs = []
    for p in range(q.shape[1] // LANES):
        qp = q[:, p * LANES:(p + 1) * LANES]
        zero = jnp.zeros_like(qp)
        qs.append(jnp.concatenate([jnp.where(low, qp, zero), jnp.where(low, zero, qp)], axis=0))
    return qs


def _causal_mask(rows, tk, heads):
    r = lax.broadcasted_iota(jnp.int32, (heads * rows, tk), 0) & (rows - 1)
    c = lax.broadcasted_iota(jnp.int32, (heads * rows, tk), 1)
    return c < r


def _attn_prompt_kernel(q_ref, k_ref, v_ref, cum_ref, o_ref, carry_ref, acc_ref, *, blk, heads):
    i = pl.program_id(1)
    low = _head_masks(blk)
    qs = _stack_heads(q_ref[...], low)
    carry_ref[...] = jnp.zeros_like(carry_ref)
    acc_ref[...] = jnp.zeros_like(acc_ref)
    cum2 = cum_ref[...]
    start = pl.multiple_of(i * blk, blk)
    _sb_block(qs, k_ref[:, pl.ds(start, blk)], v_ref[:, pl.ds(start, blk)],
              carry_ref, acc_ref, cum2, low, _causal_mask(blk, blk, 2))

    def older_block(kb):
        s = pl.multiple_of(kb * blk, blk)
        _sb_block(qs, k_ref[:, pl.ds(s, blk)], v_ref[:, pl.ds(s, blk)],
                  carry_ref, acc_ref, cum2, low, None)

    def cond(state):
        kb, live = state
        return jnp.logical_and(kb >= 1, live > ATT_DEAD)

    def body(state):
        kb, _ = state
        older_block(kb)
        older_block(kb - 1)
        return kb - 2, jnp.max(carry_ref[...])

    kb_end, live_end = lax.while_loop(cond, body, (i - 1, jnp.max(carry_ref[...])))

    @pl.when(jnp.logical_and(kb_end == 0, live_end > ATT_DEAD))
    def _():
        older_block(0)

    o_ref[...] = acc_ref[...].astype(o_ref.dtype)


def _cum_matrix(n):
    j = jnp.arange(n, dtype=jnp.int32)
    newer = (j[:, None] > j[None, :]).astype(BF16)
    return jnp.concatenate([jnp.ones((n, LANES), BF16), newer], axis=1)


def _attn_prompt(q, kb, vb, *, bsz, seq):
    m, aw = q.shape
    blk = ATT_BLOCK
    nq = seq // blk
    nh = aw // HEAD_DIM
    return pl.pallas_call(
        functools.partial(_attn_prompt_kernel, blk=blk, heads=nh),
        grid=(bsz, nq),
        in_specs=[pl.BlockSpec((blk, aw), lambda b, i: (b * nq + i, 0)),
                  pl.BlockSpec((None, aw, seq), lambda b, i: (b, 0, 0)),
                  pl.BlockSpec((None, aw, seq), lambda b, i: (b, 0, 0)),
                  _const_spec((blk, LANES + blk))],
        out_specs=pl.BlockSpec((blk, aw), lambda b, i: (b * nq + i, 0)),
        out_shape=jax.ShapeDtypeStruct((m, aw), BF16),
        scratch_shapes=[pltpu.VMEM((nh * blk, LANES), F32), pltpu.VMEM((blk, aw), F32)],
        compiler_params=_params("parallel", "arbitrary"),
        name="attn_prompt",
    )(q, kb, vb, _cum_matrix(blk))


def _attn_sample_kernel(q_ref, kn_ref, vn_ref, ck_hbm, cv_hbm, cum_new_ref, cum_ref,
                        o_ref, carry_ref, acc_ref, kbuf, vbuf, sem, *, layer, t, blk, past, heads):
    b = pl.program_id(0)
    nblk = past // blk

    def slot_of(kb):
        return lax.rem(nblk - 1 - kb, 2)

    def copies(kb):
        slot = slot_of(kb)
        cols = pl.ds(pl.multiple_of(kb * blk, blk), blk)
        return (pltpu.make_async_copy(ck_hbm.at[layer, b, :, cols], kbuf.at[slot], sem.at[0, slot]),
                pltpu.make_async_copy(cv_hbm.at[layer, b, :, cols], vbuf.at[slot], sem.at[1, slot]))

    def start(kb):
        for c in copies(kb):
            c.start()

    def wait(kb):
        for c in copies(kb):
            c.wait()

    start(nblk - 1)
    low = _head_masks(t)
    qs = _stack_heads(q_ref[...], low)
    carry_ref[...] = jnp.zeros_like(carry_ref)
    acc_ref[...] = jnp.zeros_like(acc_ref)
    _sb_block(qs, kn_ref[...], vn_ref[...], carry_ref, acc_ref,
              cum_new_ref[...], low, _causal_mask(t, t, 2))
    cum2 = cum_ref[...]

    def cond(state):
        kb, live = state
        return jnp.logical_and(kb >= 0, live > ATT_DEAD)

    def body(state):
        kb, _ = state
        wait(kb)

        @pl.when(kb > 0)
        def _():
            start(kb - 1)

        slot = slot_of(kb)
        _sb_block(qs, kbuf[slot].astype(BF16), vbuf[slot].astype(BF16),
                  carry_ref, acc_ref, cum2, low, None)
        return kb - 1, jnp.max(carry_ref[...])

    kb_end, _ = lax.while_loop(cond, body, (nblk - 1, jnp.max(carry_ref[...])))

    @pl.when(kb_end >= 0)
    def _():
        wait(kb_end)

    o_ref[...] = acc_ref[...].astype(o_ref.dtype)


def _attn_sample(q, kb, vb, ck, cv, *, layer, bsz, t, past):
    m, aw = q.shape
    blk = ATT_BLOCK
    nh = aw // HEAD_DIM
    assert past >= blk and past % blk == 0
    new = pl.BlockSpec((t, aw), lambda b: (b, 0))
    new_t = pl.BlockSpec((None, aw, t), lambda b: (b, 0, 0))
    hbm = pl.BlockSpec(memory_space=pl.ANY)
    return pl.pallas_call(
        functools.partial(_attn_sample_kernel, layer=layer, t=t, blk=blk, past=past, heads=nh),
        grid=(bsz,),
        in_specs=[new, new_t, new_t, hbm, hbm,
                  _const_spec((t, LANES + t)), _const_spec((blk, LANES + blk))],
        out_specs=new,
        out_shape=jax.ShapeDtypeStruct((m, aw), BF16),
        scratch_shapes=[pltpu.VMEM((nh * t, LANES), F32), pltpu.VMEM((t, aw), F32),
                        pltpu.VMEM((2, aw, blk), F32), pltpu.VMEM((2, aw, blk), F32),
                        pltpu.SemaphoreType.DMA((2, 2))],
        compiler_params=_params("arbitrary"),
        name="attn_sample",
    )(q, kb, vb, ck, cv, _cum_matrix(t), _cum_matrix(blk))


def _hgrn_group(blk, lbp, gain, tril, ind, bd_mask, st_ref):
    bw = gain.shape[1]
    n = HG_STEP
    half = n // 2
    steps = blk.shape[0] // n
    npair = bw // LANES
    pair = lambda p: slice(p * LANES, (p + 1) * LANES)
    qr = blk[:, 0:bw]
    z = blk[:, bw:2 * bw]
    v = blk[:, 2 * bw:3 * bw]
    gr = blk[:, 3 * bw:4 * bw]
    log_lb, log1m_lb, one_m_lb = lbp[0:1], lbp[1:2], lbp[2:3]

    sp = _softplus_neg_abs(z)
    cc = log1m_lb + (jnp.minimum(z, 0.0) - sp)
    log_f = jnp.maximum(log_lb, cc) + _softplus_neg_abs(log_lb - cc)
    k = one_m_lb * jnp.exp(jnp.minimum(-z, 0.0) - sp)
    q = qr * jax.nn.sigmoid(qr)
    b = _ldot_x3(tril, log_f)
    vb = v.astype(BF16)

    row = lax.broadcasted_iota(jnp.int32, (n, bw), 0)
    row_hi = lax.broadcasted_iota(jnp.int32, (half, bw), 0) + half
    parts, qbs, decays, upds = [], [], [], []
    for j in range(steps):
        sl = slice(j * n, (j + 1) * n)
        bj, qj, kj = b[sl], q[sl], k[sl]
        b_last = bj[n - 1:n]
        qbs.append((qj * jnp.exp(bj)).astype(BF16))
        kd = (kj * jnp.exp(b_last - bj)).astype(BF16)
        decays.append(jnp.exp(b_last))
        upds.append([lax.dot_general(vb[sl, pair(p)], kd[:, pair(p)], (((0,), (0,)), ((), ())),
                                     preferred_element_type=F32) for p in range(npair)])
        for s in range(half):
            e = jnp.exp(jnp.where(row >= s, bj - bj[s:s + 1], NEG_BIG))
            parts.append((qj * e) * kj[s:s + 1])
        bh, qh = bj[half:], qj[half:]
        for s in range(half, n):
            e = jnp.exp(jnp.where(row_hi >= s, bh - bj[s:s + 1], NEG_BIG))
            parts.append((qh * e) * kj[s:s + 1])
    att = _dot(jnp.concatenate(parts, axis=0).astype(BF16), ind)
    per_step = half * n + half * half

    sts = [st_ref[p] for p in range(npair)]
    outs = []
    for j in range(steps):
        base = j * per_step
        vj = v[j * n:(j + 1) * n]
        o_lo = att[base:base + n] * vj[0:1]
        for s in range(1, half):
            o_lo = o_lo + att[base + s * n:base + (s + 1) * n] * vj[s:s + 1]
        base += half * n
        o_hi = att[base:base + half] * vj[half:half + 1]
        for s in range(half + 1, n):
            o_hi = o_hi + att[base + (s - half) * half:base + (s - half + 1) * half] * vj[s:s + 1]
        o_intra = jnp.concatenate([o_lo[:half], o_lo[half:] + o_hi], axis=0)
        o_inter = []
        for p in range(npair):
            o_inter.append(_dot_nt(qbs[j][:, pair(p)], sts[p].astype(BF16)))
            sts[p] = sts[p] * decays[j][:, pair(p)] + jnp.where(bd_mask, upds[j][p], 0.0)
        outs.append(o_intra + jnp.concatenate(o_inter, axis=1))
    for p in range(npair):
        st_ref[p] = sts[p]
    o = jnp.concatenate(outs, axis=0) if steps > 1 else outs[0]

    ms = _dot_x2(o * o, ind) * (1.0 / HEAD_DIM)
    o = o * lax.rsqrt(ms + EPS) * gain
    return o * (gr * jax.nn.sigmoid(gr))


def _hgrn_kernel(b_ref, lbp_ref, gain_ref, tril_ref, ind_ref, st0_ref, o_ref, st_out_ref, st_ref,
                 *, groups, grows):
    i = pl.program_id(1)

    @pl.when(i == 0)
    def _():
        st_ref[...] = st0_ref[...]

    lbp = lbp_ref[...]
    gain = gain_ref[...]
    tril = tril_ref[...]
    ind = ind_ref[...]
    r = lax.broadcasted_iota(jnp.int32, (LANES, LANES), 0) // HEAD_DIM
    c = lax.broadcasted_iota(jnp.int32, (LANES, LANES), 1) // HEAD_DIM
    bd_mask = r == c

    def body(j, _):
        rows = pl.ds(pl.multiple_of(j * grows, grows), grows)
        o = _hgrn_group(b_ref[rows, :], lbp, gain, tril, ind, bd_mask, st_ref)
        o_ref[rows, :] = o.astype(o_ref.dtype)
        return 0

    lax.fori_loop(0, groups, body, 0)

    @pl.when(i == pl.num_programs(1) - 1)
    def _():
        st_out_ref[...] = st_ref[...]


def _hgrn(bproj, lbp, gain, st0, *, bsz, t, tc, gsteps=16):
    m, w4 = bproj.shape
    bw = w4 // 4
    tc = min(tc, t)
    nc = t // tc
    grows = min(gsteps * HG_STEP, tc)
    npair = bw // LANES
    j = jnp.arange(grows, dtype=jnp.int32)
    tril = jnp.logical_and(j[:, None] >= j[None, :],
                           j[:, None] // HG_STEP == j[None, :] // HG_STEP).astype(BF16)
    st_spec = pl.BlockSpec((None, npair, LANES, LANES), lambda b, i: (b, 0, 0, 0))
    return pl.pallas_call(
        functools.partial(_hgrn_kernel, groups=tc // grows, grows=grows),
        grid=(bsz, nc),
        in_specs=[pl.BlockSpec((tc, w4), lambda b, i: (b * nc + i, 0)),
                  _const_spec((3, bw)), _const_spec((1, bw)),
                  _const_spec((grows, grows)), _const_spec((bw, bw)), st_spec],
        out_specs=[pl.BlockSpec((tc, bw), lambda b, i: (b * nc + i, 0)), st_spec],
        out_shape=[jax.ShapeDtypeStruct((m, bw), BF16),
                   jax.ShapeDtypeStruct((bsz, npair, LANES, LANES), F32)],
        scratch_shapes=[pltpu.VMEM((npair, LANES, LANES), F32)],
        compiler_params=_params("parallel", "arbitrary"),
        name="hgrn2",
    )(bproj, lbp, gain, tril, _group_indicator(bw), st0)


def _state_to_tiles(s):
    bsz, h, dk, dv = s.shape
    st = jnp.swapaxes(s, 2, 3).reshape(bsz, h // 2, 2, dv, dk)
    z = jnp.zeros_like(st[:, :, 0])
    top = jnp.concatenate([st[:, :, 0], z], axis=-1)
    bot = jnp.concatenate([z, st[:, :, 1]], axis=-1)
    return jnp.concatenate([top, bot], axis=-2)


def _tiles_to_state(tiles):
    a = tiles[:, :, :HEAD_DIM, :HEAD_DIM]
    b = tiles[:, :, HEAD_DIM:, HEAD_DIM:]
    st = jnp.stack([a, b], axis=2)
    bsz, npair = tiles.shape[:2]
    return jnp.swapaxes(st.reshape(bsz, 2 * npair, HEAD_DIM, HEAD_DIM), 2, 3)


def _sgu_kernel(c_ref, gain_ref, w_ref, bias_ref, ind_ref, y_ref, vn_ref, *, cw, lc):
    c = c_ref[...]
    u = jax.nn.gelu(c[:, :cw])
    gv = jax.nn.gelu(c[:, cw:])
    vn = _group_rmsnorm(gv, ind_ref[...], gain_ref[...])
    vn_ref[...] = vn
    low = _head_masks(lc)
    bias = bias_ref[...]
    for j in range(c.shape[0] // lc):
        rows = slice(j * lc, (j + 1) * lc)
        mixed = []
        for p in range(cw // LANES):
            vp = vn[rows, p * LANES:(p + 1) * LANES].astype(BF16)
            mixed.append(jnp.where(low, _dot(w_ref[2 * p], vp), _dot(w_ref[2 * p + 1], vp)))
        y_ref[rows, :] = (u[rows] * (jnp.concatenate(mixed, axis=1) + bias)).astype(y_ref.dtype)


def _sgu(cproj, gain, w, bias, *, lc, tm):
    m, w2 = cproj.shape
    cw = w2 // 2
    ng = cw // HEAD_DIM
    tm = min(tm, m)
    row = lambda width: pl.BlockSpec((tm, width), lambda i: (i, 0))
    return pl.pallas_call(
        functools.partial(_sgu_kernel, cw=cw, lc=lc),
        grid=(m // tm,),
        in_specs=[row(w2), _const_spec((1, cw)), _const_spec((ng, lc, lc)),
                  _const_spec((lc, cw)), _const_spec((cw, cw))],
        out_specs=[row(cw), row(cw)],
        out_shape=[jax.ShapeDtypeStruct((m, cw), BF16), jax.ShapeDtypeStruct((m, cw), F32)],
        compiler_params=_params("parallel"),
        name="sgu",
    )(cproj, gain, w, bias, _group_indicator(cw))


def _mix_residual(x_ref, oa_ref, ob_ref, oc_ref, wo_ref):
    aw = oa_ref.shape[1]
    bw = ob_ref.shape[1]
    return (x_ref[...] + _dot(oa_ref[...], wo_ref[0:aw, :])
            + _dot(ob_ref[...], wo_ref[aw:aw + bw, :])
            + _dot(oc_ref[...], wo_ref[aw + bw:, :]))


def _rms(h, g):
    return h * lax.rsqrt(jnp.mean(h * h, axis=-1, keepdims=True) + EPS) * g


def _ffn_dense_kernel(x_ref, oa_ref, ob_ref, oc_ref, wo_ref, g_ref, wg_ref, wu_ref, wd_ref, y_ref,
                      *, fchunk):
    h = _mix_residual(x_ref, oa_ref, ob_ref, oc_ref, wo_ref)
    hn = _rms(h, g_ref[...]).astype(BF16)
    y_ref[...] = h
    for f0 in range(0, wg_ref.shape[1], fchunk):
        cols = slice(f0, f0 + fchunk)
        gate = _dot(hn, wg_ref[:, cols])
        up = _dot(hn, wu_ref[:, cols])
        act = (gate * jax.nn.sigmoid(gate) * up).astype(BF16)
        y_ref[...] += _dot(act, wd_ref[cols, :])


def _ffn_dense(x, oa, ob, oc, wo, g, wg, wu, wd, *, tm):
    m, d = x.shape
    tm = min(tm, m)
    ff = wg.shape[1]
    fchunk = 2 * MOE_FCHUNK if ff % (2 * MOE_FCHUNK) == 0 else (MOE_FCHUNK if ff % MOE_FCHUNK == 0 else ff)
    row = lambda a: pl.BlockSpec((tm, a.shape[1]), lambda i: (i, 0))
    return pl.pallas_call(
        functools.partial(_ffn_dense_kernel, fchunk=fchunk),
        grid=(m // tm,),
        in_specs=[row(x), row(oa), row(ob), row(oc), _const_spec(wo.shape), _const_spec((1, d)),
                  _const_spec(wg.shape), _const_spec(wu.shape), _const_spec(wd.shape)],
        out_specs=row(x),
        out_shape=jax.ShapeDtypeStruct((m, d), F32),
        compiler_params=_params("parallel"),
        name="ffn_dense",
    )(x, oa, ob, oc, wo, g.reshape(1, d), wg, wu, wd)


def _route(hn, wr):
    logits = jnp.dot(hn, wr, preferred_element_type=F32, precision=lax.Precision.HIGHEST)
    lane = lax.broadcasted_iota(jnp.int32, logits.shape, 1).astype(F32)
    logits = jnp.where(lane < N_EXPERTS, logits, -jnp.inf)
    m1 = jnp.max(logits, axis=-1, keepdims=True)
    i1 = jnp.min(jnp.where(logits == m1, lane, float(LANES)), axis=-1, keepdims=True)
    rest = jnp.where(lane == i1, -jnp.inf, logits)
    m2 = jnp.max(rest, axis=-1, keepdims=True)
    i2 = jnp.min(jnp.where(rest == m2, lane, float(LANES)), axis=-1, keepdims=True)
    e2 = jnp.exp(m2 - m1)
    g1 = 1.0 / (1.0 + e2)
    g2 = e2 / (1.0 + e2)
    return jnp.where(lane == i1, g1, 0.0) + jnp.where(lane == i2, g2, 0.0)


def _ffn_moe_kernel(x_ref, oa_ref, ob_ref, oc_ref, wo_ref, g_ref, wr_ref, wg_ref, wu_ref, wd_ref,
                    y_ref, h_ref, hn_ref, comb_ref, acc_ref):
    e = pl.program_id(1)

    @pl.when(e == 0)
    def _():
        h = _mix_residual(x_ref, oa_ref, ob_ref, oc_ref, wo_ref)
        hn = _rms(h, g_ref[...])
        h_ref[...] = h
        hn_ref[...] = hn.astype(BF16)
        comb = _route(hn, wr_ref[...])
        for j in range(N_EXPERTS):
            comb_ref[j] = jnp.broadcast_to(comb[:, j:j + 1], comb.shape)
        acc_ref[...] = jnp.zeros_like(acc_ref)

    hn = hn_ref[...]
    parts = []
    for f0 in range(0, wg_ref.shape[1], MOE_FCHUNK):
        cols = slice(f0, f0 + MOE_FCHUNK)
        gate = _dot(hn, wg_ref[:, cols])
        up = _dot(hn, wu_ref[:, cols])
        act = (gate * jax.nn.sigmoid(gate) * up).astype(BF16)
        parts.append(_dot(act, wd_ref[cols, :]))
    while len(parts) > 1:
        parts = [a + b for a, b in zip(parts[0::2], parts[1::2])] + parts[len(parts) // 2 * 2:]
    out = parts[0]
    cb = comb_ref[e]
    for j in range(out.shape[1] // LANES):
        lanes = slice(j * LANES, (j + 1) * LANES)
        acc_ref[:, lanes] += cb * out[:, lanes]

    @pl.when(e == pl.num_programs(1) - 1)
    def _():
        y_ref[...] = h_ref[...] + acc_ref[...]


def _ffn_moe(x, oa, ob, oc, wo, g, wr, wg, wu, wd, *, tm):
    m, d = x.shape
    tm = min(tm, m)
    ne, _, ff = wg.shape
    row = lambda a: pl.BlockSpec((tm, a.shape[1]), lambda i, e: (i, 0))
    const = lambda shape: pl.BlockSpec(shape, lambda i, e: (0,) * len(shape))
    return pl.pallas_call(
        _ffn_moe_kernel,
        grid=(m // tm, ne),
        in_specs=[row(x), row(oa), row(ob), row(oc), const(wo.shape), const((1, d)),
                  const(wr.shape),
                  pl.BlockSpec((None, d, ff), lambda i, e: (e, 0, 0)),
                  pl.BlockSpec((None, d, ff), lambda i, e: (e, 0, 0)),
                  pl.BlockSpec((None, ff, d), lambda i, e: (e, 0, 0))],
        out_specs=row(x),
        out_shape=jax.ShapeDtypeStruct((m, d), F32),
        scratch_shapes=[pltpu.VMEM((tm, d), F32), pltpu.VMEM((tm, d), BF16),
                        pltpu.VMEM((ne, tm, LANES), F32), pltpu.VMEM((tm, d), F32)],
        compiler_params=_params("parallel", "arbitrary"),
        name="ffn_moe",
    )(x, oa, ob, oc, wo, g.reshape(1, d), wr, wg, wu, wd)


def _tile_gain(g, width):
    return jnp.tile(g.astype(F32), width // g.shape[0]).reshape(1, width)


def kernel(x_prompt, x_sample, cache_k, cache_v, state_b, w_in, w_out, norm_mix, norm_ffn,
           q_norm, k_norm, hgrn_lb, hgrn_out_norm, sgu_norm, sgu_w, sgu_b,
           ffn_w_gate, ffn_w_up, ffn_w_down, moe_router, moe_w_gate, moe_w_up, moe_w_down):
    depth = w_in.shape[0]
    bsz, seq, d = x_prompt.shape
    dbsz, dseq, _ = x_sample.shape
    past = cache_k.shape[2]
    a_heads = cache_k.shape[3]
    b_heads = state_b.shape[2]
    aw = a_heads * HEAD_DIM
    bw = b_heads * HEAD_DIM
    c_groups = sgu_w.shape[1]
    cw = c_groups * HEAD_DIM
    c_chunk = sgu_w.shape[2]

    cs = jnp.cumsum(jax.nn.softmax(hgrn_lb.astype(F32), axis=0), axis=0)
    lb_all = cs - cs[0:1]

    ck_t = jnp.transpose(cache_k, (0, 1, 3, 4, 2)).reshape(depth, dbsz, aw, past)
    cv_t = jnp.transpose(cache_v, (0, 1, 3, 4, 2)).reshape(depth, dbsz, aw, past)

    def layer(l, xp, xs, k_all, v_all):
        w_in_l = w_in[l].astype(BF16)
        w_out_l = w_out[l].astype(BF16)
        qg = _tile_gain(q_norm[l], aw)
        kg = _tile_gain(k_norm[l], aw)
        hg = _tile_gain(hgrn_out_norm[l], bw)
        sg = _tile_gain(sgu_norm[l], cw)
        lb = lb_all[l]
        lbp = jnp.stack([jnp.log(lb), jnp.log1p(-lb), 1.0 - lb])

        def sgu_params(lc):
            tril = jnp.tril(jnp.ones((lc, lc), sgu_w.dtype))
            w = (sgu_w[l][:, :lc, :lc] * tril).astype(BF16)
            bias = jnp.repeat(sgu_b[l][:, :lc].T, HEAD_DIM, axis=1)
            return w, bias

        def ffn(x, oa, ob, oc):
            i = l // 2
            if l % 2 == 0:
                return _ffn_dense(x, oa, ob, oc, w_out_l, norm_ffn[l], ffn_w_gate[i].astype(BF16),
                                  ffn_w_up[i].astype(BF16), ffn_w_down[i].astype(BF16), tm=512)
            wr = jnp.zeros((d, LANES), F32).at[:, :N_EXPERTS].set(moe_router[i])
            return _ffn_moe(x, oa, ob, oc, w_out_l, norm_ffn[l], wr, moe_w_gate[i].astype(BF16),
                            moe_w_up[i].astype(BF16), moe_w_down[i].astype(BF16), tm=512)

        x = xp.reshape(bsz * seq, d)
        q, k_all, v_all, kb, vb, bp, cp = _inproj(
            x, norm_mix[l], w_in_l, qg, kg, aw=aw, bw=bw, cw=cw, tm=512,
            key_major=(l, depth, bsz, seq, k_all, v_all))
        oa = _attn_prompt(q, kb, vb, bsz=bsz, seq=seq)
        st0 = jnp.zeros((bsz, bw // LANES, LANES, LANES), F32)
        ob, st = _hgrn(bp, lbp, hg, st0, bsz=bsz, t=seq, tc=512)
        sw, sb = sgu_params(c_chunk)
        oc, _ = _sgu(cp, sg, sw, sb, lc=c_chunk, tm=8 * c_chunk)
        yp = ffn(x, oa, ob, oc).reshape(bsz, seq, d)
        outs_p = (_tiles_to_state(st),)

        x = xs.reshape(dbsz * dseq, d)
        q, k, v, kb, vb, bp, cp = _inproj(x, norm_mix[l], w_in_l, qg, kg, aw=aw, bw=bw, cw=cw, tm=512)
        kb = jnp.swapaxes(kb.reshape(dbsz, dseq, aw), 1, 2)
        vb = jnp.swapaxes(vb.reshape(dbsz, dseq, aw), 1, 2)
        oa = _attn_sample(q, kb, vb, ck_t, cv_t, layer=l, bsz=dbsz, t=dseq, past=past)
        ob, st = _hgrn(bp, lbp, hg, _state_to_tiles(state_b[l].astype(F32)), bsz=dbsz, t=dseq, tc=dseq)
        sw, sb = sgu_params(dseq)
        oc, vn = _sgu(cp, sg, sw, sb, lc=dseq, tm=dbsz * dseq)
        ys = ffn(x, oa, ob, oc).reshape(dbsz, dseq, d)
        outs_s = (k.reshape(dbsz, dseq, a_heads, HEAD_DIM), v.reshape(dbsz, dseq, a_heads, HEAD_DIM),
                  _tiles_to_state(st), vn.reshape(dbsz, dseq, cw))
        return yp, ys, k_all, v_all, outs_p, outs_s

    yp, ys = x_prompt, x_sample
    k_all = v_all = None
    ps, ss = [], []
    for l in range(depth):
        yp, ys, k_all, v_all, op, os_ = layer(l, yp, ys, k_all, v_all)
        ps.append(op)
        ss.append(os_)
    stack = lambda items, j: jnp.stack([it[j] for it in items])
    rows_major = lambda a: jnp.transpose(a.reshape(depth, bsz, a_heads, HEAD_DIM, seq), (0, 1, 4, 2, 3))
    return (yp, ys, rows_major(k_all), rows_major(v_all), stack(ps, 0),
            stack(ss, 0), stack(ss, 1), stack(ss, 2), stack(ss, 3))
```
